```python
import jax, jax.numpy as jnp
from jax import lax
import numpy as np

D_MODEL = 2048
BATCH = 2
SEQ = 4096
DEPTH = 2
DEC_BATCH = 128
DEC_SEQ = 4
PAST_LEN = 8192
PAGE_SIZE = 128

HEAD_DIM = 128
N_MIXERS = 4
N_HEADS = D_MODEL // HEAD_DIM
HEADS_PER_MIXER = N_HEADS // N_MIXERS
GROUP_WIDTH = HEADS_PER_MIXER * HEAD_DIM
D_MIX = N_MIXERS * GROUP_WIDTH
MLA_Q_RANK = 3 * D_MODEL // 16
MLA_KV_RANK = D_MODEL // 16
MLA_NOPE = 128
MLA_ROPE = 64
MLA_QK = MLA_NOPE + MLA_ROPE
NSA_CMP_BLOCK = 32
NSA_SEL_BLOCK = 64
NSA_TOPK = 16
NSA_WINDOW = 512
NSA_FORCED_SCORE = 1.0e4
D_FF = ((8 * D_MODEL // 3 + 127) // 128) * 128
ROPE_THETA = 10000.0
Q_BLOCK = 128
RMS_EPS = 1e-6
IN_SIZES = (MLA_Q_RANK, MLA_KV_RANK, MLA_ROPE,
            GROUP_WIDTH, 2 * HEAD_DIM, 2 * HEAD_DIM, 2 * HEAD_DIM,
            3 * HEADS_PER_MIXER,
            GROUP_WIDTH, 2 * HEAD_DIM,
            GROUP_WIDTH, 2 * HEAD_DIM, HEADS_PER_MIXER)
N_IN = sum(IN_SIZES)

kernel_name = 'hybrid_mla_nsa_stickbreak_fox_macaron_step'


def rms_norm(x, g):
    xf = x.astype(jnp.float32)
    y = xf * lax.rsqrt(jnp.mean(xf * xf, axis=-1, keepdims=True) + RMS_EPS)
    return (y * g.astype(jnp.float32)).astype(x.dtype)


def rope(x, pos):
    d = x.shape[-1]
    half = d // 2
    inv = ROPE_THETA ** (-jnp.arange(half, dtype=jnp.float32) / half)
    ang = pos.astype(jnp.float32)[:, None] * inv[None, :]
    ang = ang.reshape((1, pos.shape[0]) + (1,) * (x.ndim - 3) + (half,))
    c, s = jnp.cos(ang), jnp.sin(ang)
    xf = x.astype(jnp.float32)
    x1, x2 = xf[..., :half], xf[..., half:]
    return jnp.concatenate([x1 * c - x2 * s, x2 * c + x1 * s], axis=-1).astype(x.dtype)


def swiglu(x, w_in, w_out):
    a, b = jnp.split(x @ w_in, 2, axis=-1)
    return (jax.nn.silu(a) * b) @ w_out


def masked_softmax(s, mask):
    s = jnp.where(mask, s.astype(jnp.float32), -jnp.inf)
    m = jnp.max(s, axis=-1, keepdims=True)
    m = jnp.where(jnp.isfinite(m), m, 0.0)
    e = jnp.where(mask, jnp.exp(s - m), 0.0)
    den = jnp.sum(e, axis=-1, keepdims=True)
    return e / jnp.where(den > 0, den, 1.0)


def over_query_blocks(fn, q_args, qpos):
    tq = qpos.shape[0]
    if tq <= Q_BLOCK:
        return fn(*q_args, qpos)
    nb = tq // Q_BLOCK

    def split(a):
        return jnp.moveaxis(a.reshape((a.shape[0], nb, Q_BLOCK) + a.shape[2:]), 1, 0)

    xs = tuple(split(a) for a in q_args) + (qpos.reshape(nb, Q_BLOCK),)
    out = lax.map(lambda t: fn(*t), xs)
    out = jnp.moveaxis(out, 0, 1)
    return out.reshape((out.shape[0], tq) + out.shape[3:])


def mqa_attend(q, k, v, qpos, kpos, scale, cq=None, ck=None):
    s = jnp.einsum('bqhd,bkd->bhqk', q, k).astype(jnp.float32) * scale
    if cq is not None:
        s = s + jnp.transpose(cq, (0, 2, 1))[..., None] - jnp.transpose(ck, (0, 2, 1))[:, :, None, :]
    p = masked_softmax(s, (kpos[None, :] <= qpos[:, None])[None, None])
    return jnp.einsum('bhqk,bkd->bqhd', p, v.astype(jnp.float32))


def stick_breaking_attend(q, k, v, qpos, kpos):
    z = jnp.einsum('bqhd,bkd->bhqk', q, k).astype(jnp.float32) * (q.shape[-1] ** -0.5)
    mask = (kpos[None, :] < qpos[:, None])[None, None]
    log_fail = jnp.where(mask, jax.nn.log_sigmoid(-z), 0.0)
    log_between = lax.cumsum(log_fail, axis=3, reverse=True) - log_fail
    a = jnp.where(mask, jnp.exp(jax.nn.log_sigmoid(z) + log_between), 0.0)
    return jnp.einsum('bhqk,bkd->bqhd', a, v.astype(jnp.float32))


def compress_blocks(rows):
    b, l, c = rows.shape
    lp = -(-l // NSA_SEL_BLOCK) * NSA_SEL_BLOCK
    rows = jnp.pad(rows, ((0, 0), (0, lp - l), (0, 0)))
    m = rows.reshape(b, lp // NSA_CMP_BLOCK, NSA_CMP_BLOCK, c).astype(jnp.float32).mean(axis=2)
    return m[..., :HEAD_DIM], m[..., HEAD_DIM:]


def nsa_attend(q, g, qpos, cmp_k, cmp_v, cmp_end, gather_sel, window_kv):
    scale = HEAD_DIM ** -0.5
    s_c = jnp.einsum('bqhd,bcd->bhqc', q, cmp_k) * scale
    p_c = masked_softmax(s_c, (cmp_end[None, :] <= qpos[:, None])[None, None])
    o_c = jnp.einsum('bhqc,bcd->bqhd', p_c, cmp_v)
    b, _, tq, nc = p_c.shape
    ratio = NSA_SEL_BLOCK // NSA_CMP_BLOCK
    nsb = nc // ratio
    imp = jnp.sum(p_c, axis=1).reshape(b, tq, nsb, ratio).sum(axis=-1)
    blk = jnp.arange(nsb)[None, :]
    cur = (qpos // NSA_SEL_BLOCK)[:, None]
    forced = (blk == 0) | (blk == cur) | (blk == cur - 1)
    future = blk * NSA_SEL_BLOCK > qpos[:, None]
    score = jnp.where(forced, NSA_FORCED_SCORE, jnp.where(future, -1.0, imp))
    _, idx = lax.top_k(score, min(NSA_TOPK, nsb))
    rows, spos = gather_sel(idx)
    s_s = jnp.einsum('bqhd,bqnd->bqhn', q, rows[..., :HEAD_DIM]) * scale
    p_s = masked_softmax(s_s, (spos <= qpos[None, :, None])[:, :, None, :])
    o_s = jnp.einsum('bqhn,bqnd->bqhd', p_s, rows[..., HEAD_DIM:].astype(jnp.float32))
    wrows, wpos = window_kv(qpos)
    s_w = jnp.einsum('bqhd,bkd->bhqk', q, wrows[..., :HEAD_DIM]) * scale
    diff = qpos[:, None] - wpos[None, :]
    wmask = (diff >= 0) & (diff < NSA_WINDOW) & (wpos[None, :] >= 0)
    p_w = masked_softmax(s_w, wmask[None, None])
    o_w = jnp.einsum('bhqk,bkd->bqhd', p_w, wrows[..., HEAD_DIM:].astype(jnp.float32))
    return g[..., 0:1] * o_c + g[..., 1:2] * o_s + g[..., 2:3] * o_w


def local_sel_gather(rows):
    b, l, c = rows.shape
    blocks = rows.reshape(b, l // NSA_SEL_BLOCK, NSA_SEL_BLOCK, c)

    def gather(idx):
        got = jax.vmap(lambda r, i: r[i])(blocks, idx)
        pos = idx[..., None] * NSA_SEL_BLOCK + jnp.arange(NSA_SEL_BLOCK)
        bb, tq, k = idx.shape
        return got.reshape(bb, tq, k * NSA_SEL_BLOCK, c), pos.reshape(bb, tq, k * NSA_SEL_BLOCK)
    return gather


def band_window(rows):
    l = rows.shape[1]
    padded = jnp.pad(rows, ((0, 0), (NSA_WINDOW, 0), (0, 0)))
    ppos = jnp.arange(l + NSA_WINDOW, dtype=jnp.int32) - NSA_WINDOW

    def window(qpos):
        n = NSA_WINDOW + qpos.shape[0]
        start = qpos[0]
        return (lax.dynamic_slice_in_dim(padded, start, n, axis=1),
                lax.dynamic_slice_in_dim(ppos, start, n, axis=0))
    return window


def gather_past(pool, page_table):
    got = pool[page_table]
    return got.reshape(got.shape[0], -1, got.shape[-1])


def paged_sel_gather(pool, page_table, new_rows, idx):
    db, tq, k = idx.shape
    c = pool.shape[-1]
    sub = PAGE_SIZE // NSA_SEL_BLOCK
    n_past_blk = page_table.shape[1] * sub
    pool_blk = pool.reshape(pool.shape[0] * sub, NSA_SEL_BLOCK, c)
    ic = jnp.clip(idx, 0, n_past_blk - 1)
    phys = jax.vmap(lambda pt, i: pt[i // sub])(page_table, ic)
    past = pool_blk[phys * sub + ic % sub]
    t_new = new_rows.shape[1]
    n_new = -(-t_new // NSA_SEL_BLOCK)
    new_blk = jnp.pad(new_rows, ((0, 0), (0, n_new * NSA_SEL_BLOCK - t_new), (0, 0)))
    new_blk = new_blk.reshape(db, n_new, NSA_SEL_BLOCK, c)
    inew = jnp.clip(idx - n_past_blk, 0, n_new - 1)
    new = jax.vmap(lambda nb, i: nb[i])(new_blk, inew)
    got = jnp.where((idx < n_past_blk)[..., None, None], past, new)
    pos = idx[..., None] * NSA_SEL_BLOCK + jnp.arange(NSA_SEL_BLOCK)
    return got.reshape(db, tq, k * NSA_SEL_BLOCK, c), pos.reshape(db, tq, k * NSA_SEL_BLOCK)


def split_columns(z):
    out, start = [], 0
    for size in IN_SIZES:
        out.append(z[..., start:start + size])
        start += size
    return out


def project(hn, pos, lw):
    b, t, _ = hn.shape
    h = HEADS_PER_MIXER
    (dq, dkv, kr, nq, ncmp, nsel, nwin, ngate, sq, skv, fq, fkv, fg) = split_columns(hn @ lw['w_in'])
    q = (rms_norm(dq, lw['mla_q_norm']) @ lw['mla_w_uq']).reshape(b, t, h, MLA_QK)
    q = rms_norm(q, lw['mla_qk_norm'])
    q_lat = jnp.einsum('bthn,hrn->bthr', q[..., :MLA_NOPE], lw['mla_w_uk'])
    mla_q = jnp.concatenate([q_lat, rope(q[..., MLA_NOPE:], pos).astype(q_lat.dtype)], axis=-1)
    mla_row = jnp.concatenate([rms_norm(dkv, lw['mla_kv_norm']),
                               rope(rms_norm(kr, lw['mla_kr_norm']), pos)], axis=-1)

    def nsa_row(kv, i):
        k = rope(rms_norm(kv[..., :HEAD_DIM], lw['nsa_k_norm'][i]), pos)
        return jnp.concatenate([k, kv[..., HEAD_DIM:]], axis=-1)

    nsa_q = rope(rms_norm(nq.reshape(b, t, h, HEAD_DIM), lw['nsa_q_norm']), pos)
    nsa_g = jax.nn.sigmoid(ngate.astype(jnp.float32)).reshape(b, t, h, 3)
    sb_q = sq.reshape(b, t, h, HEAD_DIM)
    fox_q = rms_norm(fq.reshape(b, t, h, HEAD_DIM), lw['fox_q_norm'])
    fox_row = jnp.concatenate([rms_norm(fkv[..., :HEAD_DIM], lw['fox_k_norm']), fkv[..., HEAD_DIM:]], axis=-1)
    fox_logf = jax.nn.log_sigmoid((fg + lw['fox_f_bias']).astype(jnp.float32))
    return dict(mla_q=mla_q, mla_row=mla_row, nsa_q=nsa_q, nsa_g=nsa_g,
                nsa_cmp_row=nsa_row(ncmp, 0), nsa_sel_row=nsa_row(nsel, 1), nsa_win_row=nsa_row(nwin, 2),
                sb_q=sb_q, sb_row=skv, fox_q=fox_q, fox_row=fox_row, fox_logf=fox_logf)


def token_mixers(pr, qpos, kpos, mla_rows, cmp_rows, gather_sel, window_kv, sb_rows, fox_rows, fox_logf, lw):
    mla_v = mla_rows[..., :MLA_KV_RANK]
    o_lat = over_query_blocks(
        lambda q, qp: mqa_attend(q, mla_rows, mla_v, qp, kpos, MLA_QK ** -0.5), (pr['mla_q'],), qpos)
    o_mla = jnp.einsum('bthr,hrv->bthv', o_lat, lw['mla_w_uv'].astype(jnp.float32))
    cmp_k, cmp_v = compress_blocks(cmp_rows)
    cmp_end = jnp.arange(cmp_k.shape[1], dtype=jnp.int32) * NSA_CMP_BLOCK + (NSA_CMP_BLOCK - 1)
    o_nsa = over_query_blocks(
        lambda q, g, qp: nsa_attend(q, g, qp, cmp_k, cmp_v, cmp_end, gather_sel, window_kv),
        (pr['nsa_q'], pr['nsa_g']), qpos)
    sb_k, sb_v = sb_rows[..., :HEAD_DIM], sb_rows[..., HEAD_DIM:]
    o_sb = over_query_blocks(
        lambda q, qp: stick_breaking_attend(q, sb_k, sb_v, qp, kpos), (pr['sb_q'],), qpos)
    cum = jnp.cumsum(fox_logf.astype(jnp.float32), axis=1)
    tq = qpos.shape[0]
    fk, fv = fox_rows[..., :HEAD_DIM], fox_rows[..., HEAD_DIM:]
    o_fox = over_query_blocks(
        lambda q, cq, qp: mqa_attend(q, fk, fv, qp, kpos, HEAD_DIM ** -0.5, cq, cum),
        (pr['fox_q'], cum[:, -tq:]), qpos)
    o = jnp.concatenate([o_mla, o_nsa, o_sb, o_fox], axis=2)
    b, t = o.shape[:2]
    return o.reshape(b, t, D_MIX).astype(lw['w_out'].dtype) @ lw['w_out']


def prompt_mixers(hn, pos, lw):
    pr = project(hn, pos, lw)
    win = pr['nsa_win_row']
    out = token_mixers(pr, pos, pos, pr['mla_row'], pr['nsa_cmp_row'], local_sel_gather(pr['nsa_sel_row']),
                       band_window(win), pr['sb_row'], pr['fox_row'], pr['fox_logf'], lw)
    wp = min(NSA_WINDOW, hn.shape[1])
    new = (pr['mla_row'], pr['nsa_cmp_row'], pr['nsa_sel_row'], win[:, -wp:],
           pr['sb_row'], pr['fox_row'], pr['fox_logf'])
    return out, new


def sample_mixers(hn, pos, lw, c_mla, c_cmp, c_sel, s_win, c_sb, c_fkv, c_flogf, page_table):
    pr = project(hn, pos, lw)
    past_len = page_table.shape[1] * PAGE_SIZE
    tk = past_len + hn.shape[1]
    kpos = jnp.arange(tk, dtype=jnp.int32)

    def with_past(pool, new_rows):
        return jnp.concatenate([gather_past(pool, page_table), new_rows], axis=1)

    wb = s_win.shape[1]
    win_rows = jnp.concatenate([s_win, pr['nsa_win_row']], axis=1)
    wpos = jnp.arange(past_len - wb, tk, dtype=jnp.int32)
    out = token_mixers(pr, pos, kpos,
                       with_past(c_mla, pr['mla_row']),
                       with_past(c_cmp, pr['nsa_cmp_row']),
                       lambda idx: paged_sel_gather(c_sel, page_table, pr['nsa_sel_row'], idx),
                       lambda qp: (win_rows, wpos),
                       with_past(c_sb, pr['sb_row']),
                       with_past(c_fkv, pr['fox_row']),
                       with_past(c_flogf, pr['fox_logf']),
                       lw)
    new = (pr['mla_row'], pr['nsa_cmp_row'], pr['nsa_sel_row'], win_rows[:, -wb:],
           pr['sb_row'], pr['fox_row'], pr['fox_logf'])
    return out, new


def setup_inputs(seed: int = 0) -> dict:
    key = jax.random.key(seed)
    k = jax.random.split(key, 28)
    h = HEADS_PER_MIXER

    def nrm(i, shape, scale=1.0):
        return jax.random.normal(k[i], shape, jnp.float32) * scale

    def gain(i, shape):
        return 1.0 + 0.02 * nrm(i, shape)

    n_pages = PAST_LEN // PAGE_SIZE
    n_used = DEC_BATCH * n_pages
    n_pool = n_used + n_used // 4
    page_table = jax.random.permutation(k[2], n_pool)[:n_used].reshape(DEC_BATCH, n_pages).astype(jnp.int32)
    win_buf = min(NSA_WINDOW, PAST_LEN)
    pool = (DEPTH, n_pool, PAGE_SIZE)
    return {
        'x_prompt': nrm(0, (BATCH, SEQ, D_MODEL)),
        'x_sample': nrm(1, (DEC_BATCH, DEC_SEQ, D_MODEL)),
        'cache_mla': nrm(3, pool + (MLA_KV_RANK + MLA_ROPE,)),
        'cache_nsa_cmp': nrm(4, pool + (2 * HEAD_DIM,)),
        'cache_nsa_sel': nrm(5, pool + (2 * HEAD_DIM,)),
        'state_nsa_win': nrm(6, (DEPTH, DEC_BATCH, win_buf, 2 * HEAD_DIM)),
        'cache_sb': nrm(7, pool + (2 * HEAD_DIM,)),
        'cache_fox_kv': nrm(8, pool + (2 * HEAD_DIM,)),
        'cache_fox_logf': jax.nn.log_sigmoid(1.0 + 0.5 * nrm(9, pool + (h,))),
        'page_table': page_table,
        'attn_norm': gain(10, (DEPTH, D_MODEL)),
        'ffn_norm': gain(11, (DEPTH, 2, D_MODEL)),
        'ffn_w_in': nrm(12, (DEPTH, 2, D_MODEL, 2 * D_FF), D_MODEL ** -0.5),
        'ffn_w_out': nrm(13, (DEPTH, 2, D_FF, D_MODEL), D_FF ** -0.5),
        'w_in': nrm(14, (DEPTH, D_MODEL, N_IN), D_MODEL ** -0.5),
        'w_out': nrm(15, (DEPTH, D_MIX, D_MODEL), D_MIX ** -0.5),
        'mla_q_norm': gain(16, (DEPTH, MLA_Q_RANK)),
        'mla_w_uq': nrm(17, (DEPTH, MLA_Q_RANK, h * MLA_QK), MLA_Q_RANK ** -0.5),
        'mla_qk_norm': gain(18, (DEPTH, MLA_QK)),
        'mla_kv_norm': gain(19, (DEPTH, MLA_KV_RANK)),
        'mla_kr_norm': gain(20, (DEPTH, MLA_ROPE)),
        'mla_w_uk': nrm(21, (DEPTH, h, MLA_KV_RANK, MLA_NOPE), MLA_NOPE ** -0.5),
        'mla_w_uv': nrm(22, (DEPTH, h, MLA_KV_RANK, HEAD_DIM), MLA_KV_RANK ** -0.5),
        'nsa_q_norm': gain(23, (DEPTH, HEAD_DIM)),
        'nsa_k_norm': gain(24, (DEPTH, 3, HEAD_DIM)),
        'fox_q_norm': gain(25, (DEPTH, HEAD_DIM)),
        'fox_k_norm': gain(26, (DEPTH, HEAD_DIM)),
        'fox_f_bias': 1.0 + 0.1 * nrm(27, (DEPTH, h)),
    }


def reference(x_prompt, x_sample, cache_mla, cache_nsa_cmp, cache_nsa_sel, state_nsa_win, cache_sb,
              cache_fox_kv, cache_fox_logf, page_table, attn_norm, ffn_norm, ffn_w_in, ffn_w_out, w_in, w_out,
              mla_q_norm, mla_w_uq, mla_qk_norm, mla_kv_norm, mla_kr_norm, mla_w_uk, mla_w_uv,
              nsa_q_norm, nsa_k_norm, fox_q_norm, fox_k_norm, fox_f_bias):
    pos_p = jnp.arange(x_prompt.shape[1], dtype=jnp.int32)
    past_len = page_table.shape[1] * PAGE_SIZE
    pos_s = past_len + jnp.arange(x_sample.shape[1], dtype=jnp.int32)
    hp, hs = x_prompt, x_sample
    st_p = [[] for _ in range(7)]
    st_s = [[] for _ in range(7)]
    for l in range(DEPTH):
        lw = dict(w_in=w_in[l], w_out=w_out[l], mla_q_norm=mla_q_norm[l], mla_w_uq=mla_w_uq[l],
                  mla_qk_norm=mla_qk_norm[l], mla_kv_norm=mla_kv_norm[l], mla_kr_norm=mla_kr_norm[l],
                  mla_w_uk=mla_w_uk[l], mla_w_uv=mla_w_uv[l], nsa_q_norm=nsa_q_norm[l],
                  nsa_k_norm=nsa_k_norm[l], fox_q_norm=fox_q_norm[l], fox_k_norm=fox_k_norm[l],
                  fox_f_bias=fox_f_bias[l])
        hp = hp + 0.5 * swiglu(rms_norm(hp, ffn_norm[l, 0]), ffn_w_in[l, 0], ffn_w_out[l, 0])
        hs = hs + 0.5 * swiglu(rms_norm(hs, ffn_norm[l, 0]), ffn_w_in[l, 0], ffn_w_out[l, 0])
        o_p, new_p = prompt_mixers(rms_norm(hp, attn_norm[l]), pos_p, lw)
        o_s, new_s = sample_mixers(rms_norm(hs, attn_norm[l]), pos_s, lw, cache_mla[l], cache_nsa_cmp[l],
                                   cache_nsa_sel[l], state_nsa_win[l], cache_sb[l], cache_fox_kv[l],
                                   cache_fox_logf[l], page_table)
        hp = hp + o_p
        hs = hs + o_s
        hp = hp + 0.5 * swiglu(rms_norm(hp, ffn_norm[l, 1]), ffn_w_in[l, 1], ffn_w_out[l, 1])
        hs = hs + 0.5 * swiglu(rms_norm(hs, ffn_norm[l, 1]), ffn_w_in[l, 1], ffn_w_out[l, 1])
        for lst, a in zip(st_p, new_p):
            lst.append(a)
        for lst, a in zip(st_s, new_s):
            lst.append(a)
    mla_p, cmp_p, sel_p, win_p, sb_p, fkv_p, flogf_p = [jnp.stack(s, axis=0) for s in st_p]
    mla_s, cmp_s, sel_s, win_s, sb_s, fkv_s, flogf_s = [jnp.stack(s, axis=0) for s in st_s]
    return (hp, hs, mla_p, mla_s, cmp_p, cmp_s, sel_p, sel_s, win_p, win_s, sb_p, sb_s, fkv_p, fkv_s, flogf_p, flogf_s)
```

```python
import functools

import jax
import jax.numpy as jnp
import numpy as np
from jax import lax
from jax.experimental import pallas as pl
from jax.experimental.pallas import tpu as pltpu

F32 = jnp.float32
BF16 = jnp.bfloat16

HEAD_DIM = 128
N_HEADS_GROUP = 4
GROUP_WIDTH = N_HEADS_GROUP * HEAD_DIM
MLA_Q_RANK = 384
MLA_KV_RANK = 128
MLA_NOPE = 128
MLA_ROPE = 64
MLA_QK = MLA_NOPE + MLA_ROPE
MLA_PAD = 256
NSA_CMP_BLOCK = 32
NSA_SEL_BLOCK = 64
NSA_TOPK = 16
NSA_WINDOW = 512
NSA_FORCED_SCORE = 1.0e4
ROPE_THETA = 10000.0
RMS_EPS = 1e-6
PAGE_SIZE = 128
NEG = -1.0e30
MASK_CUT = -1.0e29

LANES = 128
DEC_ROWS = 8
VMEM_LIMIT = 56 * 1024 * 1024


def _cparams(*sem):
    return pltpu.CompilerParams(dimension_semantics=sem, vmem_limit_bytes=VMEM_LIMIT)


def _bdot(a, b):
    return jnp.dot(a.astype(BF16), b.astype(BF16), preferred_element_type=F32)


def _bdot_nt(a, b):
    return lax.dot_general(a.astype(BF16), b.astype(BF16), (((1,), (1,)), ((), ())),
                           preferred_element_type=F32)


def _split_dot(a, b01):
    hi = a.astype(BF16)
    r1 = a - hi.astype(F32)
    mid = r1.astype(BF16)
    lo = (r1 - mid.astype(F32)).astype(BF16)
    dot = functools.partial(jnp.dot, preferred_element_type=F32)
    return dot(hi, b01) + dot(mid, b01) + dot(lo, b01)


def _rms(x, g, n=None):
    n = x.shape[-1] if n is None else n
    ms = jnp.sum(x * x, axis=-1, keepdims=True) * (1.0 / n)
    return x * lax.rsqrt(ms + RMS_EPS) * g


def _log_sigmoid(x):
    return jnp.minimum(x, 0.0) - jnp.log1p(jnp.exp(-jnp.abs(x)))


def _ffn_body(x_ref, g_ref, wa_ref, wb_ref, wo_ref, o_ref, xn_ref, acc_ref):
    j = pl.program_id(1)

    @pl.when(j == 0)
    def _():
        xn_ref[...] = _rms(x_ref[...], g_ref[...]).astype(BF16)
        acc_ref[...] = jnp.zeros_like(acc_ref)

    xn = xn_ref[...]
    a = jnp.dot(xn, wa_ref[...], preferred_element_type=F32)
    b = jnp.dot(xn, wb_ref[...], preferred_element_type=F32)
    act = (a * jax.nn.sigmoid(a) * b).astype(BF16)
    acc_ref[...] += jnp.dot(act, wo_ref[...], preferred_element_type=F32)

    @pl.when(j == pl.num_programs(1) - 1)
    def _():
        o_ref[...] = x_ref[...] + 0.5 * acc_ref[...]


def _ffn(x, g, wa, wb, wo, tm, tf):
    t, d = x.shape
    ff = wa.shape[1]
    return pl.pallas_call(
        _ffn_body,
        grid=(t // tm, ff // tf),
        in_specs=[
            pl.BlockSpec((tm, d), lambda i, j: (i, 0)),
            pl.BlockSpec((1, d), lambda i, j: (0, 0)),
            pl.BlockSpec((d, tf), lambda i, j: (0, j)),
            pl.BlockSpec((d, tf), lambda i, j: (0, j)),
            pl.BlockSpec((tf, d), lambda i, j: (j, 0)),
        ],
        out_specs=pl.BlockSpec((tm, d), lambda i, j: (i, 0)),
        out_shape=jax.ShapeDtypeStruct((t, d), F32),
        scratch_shapes=[pltpu.VMEM((tm, d), BF16), pltpu.VMEM((tm, d), F32)],
        compiler_params=_cparams("parallel", "arbitrary"),
        name="ffn",
    )(x, g, wa, wb, wo)


_C_DQ = 0
_C_DKV = 384
_C_KR = 512
_C_NQ = 640
_C_NCMP = 1152
_C_NSEL = 1408
_C_NWIN = 1664
_C_SQ = 1920
_C_SKV = 2432
_C_FQ = 2688
_C_FKV = 3200
_C_GATE = 3456
N_IN_PAD = 3584


def _rope128(x, c, s):
    return x * c + pltpu.roll(x, 64, 1) * s


def _rope64(x, c, sa, sb):
    return x * c + pltpu.roll(x, 96, 1) * sa + pltpu.roll(x, 32, 1) * sb


def _proj_body(x_ref, gin_ref, win_ref, wuq_ref, wuk_ref, vec_ref, fb_ref,
               c1_ref, s1_ref, c2_ref, s2a_ref, s2b_ref,
               mlaq_ref, mlarow_ref, nsaq_ref, cmp_ref, sel_ref, win_ref_o, sbq_ref, sbrow_ref,
               foxq_ref, foxrow_ref, gate_ref, logf_ref):
    xn = _rms(x_ref[...], gin_ref[...]).astype(BF16)
    z = jnp.dot(xn, win_ref[...], preferred_element_type=F32)
    c1, s1 = c1_ref[...], s1_ref[...]
    c2, s2a, s2b = c2_ref[...], s2a_ref[...], s2b_ref[...]
    vec = vec_ref[...]

    def gain(row, width=HEAD_DIM):
        return vec[row:row + 1, :width]

    dqn = _rms(z[:, _C_DQ:_C_DQ + MLA_Q_RANK], gain(0, MLA_Q_RANK))
    qh = _bdot(dqn, wuq_ref[...])
    for h in range(N_HEADS_GROUP):
        a = qh[:, h * MLA_PAD:h * MLA_PAD + MLA_NOPE]
        b = qh[:, h * MLA_PAD + MLA_NOPE:(h + 1) * MLA_PAD]
        ms = (jnp.sum(a * a, axis=-1, keepdims=True) + jnp.sum(b * b, axis=-1, keepdims=True)) * (1.0 / MLA_QK)
        rs = lax.rsqrt(ms + RMS_EPS)
        mlaq_ref[:, h * MLA_PAD:h * MLA_PAD + MLA_NOPE] = _bdot(a * rs * gain(1), wuk_ref[h])
        mlaq_ref[:, h * MLA_PAD + MLA_NOPE:(h + 1) * MLA_PAD] = _rope64(b * rs * gain(2), c2, s2a, s2b)
    mlarow_ref[:, :MLA_KV_RANK] = _rms(z[:, _C_DKV:_C_DKV + MLA_KV_RANK], gain(3))
    mlarow_ref[:, MLA_KV_RANK:] = _rope64(_rms(z[:, _C_KR:_C_KR + LANES], gain(4), MLA_ROPE), c2, s2a, s2b)
    for h in range(N_HEADS_GROUP):
        sl = slice(h * HEAD_DIM, (h + 1) * HEAD_DIM)
        nsaq_ref[:, sl] = _rope128(_rms(z[:, _C_NQ + h * HEAD_DIM:_C_NQ + (h + 1) * HEAD_DIM], gain(5)), c1, s1)
        sbq_ref[:, sl] = z[:, _C_SQ + h * HEAD_DIM:_C_SQ + (h + 1) * HEAD_DIM]
        foxq_ref[:, sl] = _rms(z[:, _C_FQ + h * HEAD_DIM:_C_FQ + (h + 1) * HEAD_DIM], gain(9))
    for i, (col, ref) in enumerate(((_C_NCMP, cmp_ref), (_C_NSEL, sel_ref), (_C_NWIN, win_ref_o))):
        ref[:, :HEAD_DIM] = _rope128(_rms(z[:, col:col + HEAD_DIM], gain(6 + i)), c1, s1)
        ref[:, HEAD_DIM:] = z[:, col + HEAD_DIM:col + 2 * HEAD_DIM]
    sbrow_ref[...] = z[:, _C_SKV:_C_SKV + 2 * HEAD_DIM]
    foxrow_ref[:, :HEAD_DIM] = _rms(z[:, _C_FKV:_C_FKV + HEAD_DIM], gain(10))
    foxrow_ref[:, HEAD_DIM:] = z[:, _C_FKV + HEAD_DIM:_C_FKV + 2 * HEAD_DIM]
    gt = z[:, _C_GATE:_C_GATE + LANES]
    gate_ref[...] = jax.nn.sigmoid(gt)
    logf_ref[...] = _log_sigmoid(gt + fb_ref[...])


_PROJ_OUT_WIDTHS = (4 * MLA_PAD, MLA_PAD, GROUP_WIDTH, 256, 256, 256, GROUP_WIDTH, 256, GROUP_WIDTH, 256,
                    LANES, LANES)


def _proj(x, gin, win, wuq, wuk, vec, fb, tabs, tm):
    t, d = x.shape
    row = lambda w: pl.BlockSpec((tm, w), lambda i: (i, 0))
    full = lambda a: pl.BlockSpec(a.shape, lambda i: (0,) * a.ndim)
    return pl.pallas_call(
        _proj_body,
        grid=(t // tm,),
        in_specs=[row(d), full(gin), full(win), full(wuq), full(wuk), full(vec), full(fb)]
                 + [row(LANES)] * 5,
        out_specs=[row(w) for w in _PROJ_OUT_WIDTHS],
        out_shape=[jax.ShapeDtypeStruct((t, w), F32) for w in _PROJ_OUT_WIDTHS],
        compiler_params=_cparams("parallel"),
        name="proj",
    )(x, gin, win, wuq, wuk, vec, fb, *tabs)


def _outproj_body(h_ref, olat_ref, oc_ref, os_ref, ow_ref, gate_ref, osb_ref, ofox_ref, wuv_ref, wout_ref, o_ref):
    acc = h_ref[...]
    gate = gate_ref[...]
    olat, oc, os_, ow = olat_ref[...], oc_ref[...], os_ref[...], ow_ref[...]
    mla, nsa = [], []
    for h in range(N_HEADS_GROUP):
        sl = slice(h * HEAD_DIM, (h + 1) * HEAD_DIM)
        mla.append(_bdot(olat[:, sl], wuv_ref[h]))
        nsa.append(gate[:, 3 * h:3 * h + 1] * oc[:, sl] + gate[:, 3 * h + 1:3 * h + 2] * os_[:, sl]
                   + gate[:, 3 * h + 2:3 * h + 3] * ow[:, sl])
    groups = (jnp.concatenate(mla, axis=1), jnp.concatenate(nsa, axis=1), osb_ref[...], ofox_ref[...])
    for g, og in enumerate(groups):
        acc = acc + _bdot(og, wout_ref[g * GROUP_WIDTH:(g + 1) * GROUP_WIDTH, :])
    o_ref[...] = acc


def _outproj(h, olat, oc, os_, ow, gate, osb, ofox, wuv, wout, tm):
    t, d = h.shape
    row = lambda w: pl.BlockSpec((tm, w), lambda i: (i, 0))
    full = lambda a: pl.BlockSpec(a.shape, lambda i: (0,) * a.ndim)
    return pl.pallas_call(
        _outproj_body,
        grid=(t // tm,),
        in_specs=[row(d)] + [row(GROUP_WIDTH)] * 4 + [row(LANES)] + [row(GROUP_WIDTH)] * 2 + [full(wuv), full(wout)],
        out_specs=row(d),
        out_shape=jax.ShapeDtypeStruct((t, d), F32),
        compiler_params=_cparams("parallel"),
        name="outproj",
    )(h, olat, oc, os_, ow, gate, osb, ofox, wuv, wout)


def _split2_dot(a, b01):
    hi = a.astype(BF16)
    lo = (a - hi.astype(F32)).astype(BF16)
    return (jnp.dot(hi, b01, preferred_element_type=F32) + jnp.dot(lo, b01, preferred_element_type=F32))


def _pool_dot(p01, x):
    hi = x.astype(BF16)
    lo = (x - hi.astype(F32)).astype(BF16)
    return (jnp.dot(p01, hi, preferred_element_type=F32) + jnp.dot(p01, lo, preferred_element_type=F32))


def _softplus(z):
    return jnp.maximum(z, 0.0) + jnp.log1p(jnp.exp(-jnp.abs(z)))


def _softmax_update(s, valid, v, m_ref, l_ref, acc_ref):
    n = s.shape[-1]
    s = jnp.where(valid, s, NEG)
    m_prev = m_ref[...]
    m_new = jnp.maximum(m_prev, jnp.max(s, axis=-1, keepdims=True))
    alpha = jnp.exp(m_prev - m_new)
    p = jnp.where(valid, jnp.exp(s - m_new), 0.0)
    l_ref[...] = alpha * l_ref[...] + jnp.sum(p, axis=-1, keepdims=True)
    rows = acc_ref.shape[0]
    acc_ref[...] = acc_ref[...] * alpha.reshape(rows, 1) + jnp.dot(
        p.reshape(rows, n).astype(BF16), v, preferred_element_type=F32)
    m_ref[...] = m_new


def _stick_update(z, valid, v, tri, carry_ref, acc_ref):
    sp = _softplus(z)
    lf = jnp.where(valid, -sp, 0.0)
    between = _split2_dot(lf, tri) + carry_ref[...]
    a = jnp.where(valid, jnp.exp(z - sp + between), 0.0)
    acc_ref[...] += jnp.dot(a.astype(BF16), v, preferred_element_type=F32)
    carry_ref[...] += jnp.sum(lf, axis=-1, keepdims=True)


def _tri_later(n):
    r = lax.broadcasted_iota(jnp.int32, (n, n), 0)
    c = lax.broadcasted_iota(jnp.int32, (n, n), 1)
    return (r > c).astype(BF16)


def _attn_prompt_body(*refs, tq, dq, dv, kcols, vcols, scale, mode, window, has_bias, has_sel):
    it = iter(refs)
    q_ref, kv_ref = next(it), next(it)
    rb_ref = next(it) if has_bias else None
    cb_ref = next(it) if has_bias else None
    sel_ref = next(it) if has_sel else None
    o_ref = next(it)
    acc_ref, m_ref, l_ref = next(it), next(it), next(it)
    nh = N_HEADS_GROUP
    i = pl.program_id(1)
    q = jnp.concatenate([q_ref[0, :, h * dq:(h + 1) * dq] for h in range(nh)], axis=0).astype(BF16)
    acc_ref[...] = jnp.zeros_like(acc_ref)
    l_ref[...] = jnp.zeros_like(l_ref)
    m_ref[...] = jnp.full_like(m_ref, NEG)
    delta = lax.broadcasted_iota(jnp.int32, (tq, tq), 0) - lax.broadcasted_iota(jnp.int32, (tq, tq), 1)
    lo = 0 if window is None else jnp.maximum(i - window // tq, 0)
    selm = sel_ref[0].astype(BF16) if has_sel else None
    tri = _tri_later(tq) if mode == "stick" else None

    def body(jj, carry):
        j = i - jj
        off = pl.multiple_of(j * tq, tq)
        kv = kv_ref[0, pl.ds(off, tq), :]
        k = kv[:, kcols[0]:kcols[1]].astype(BF16)
        v = kv[:, vcols[0]:vcols[1]].astype(BF16)
        s = _bdot_nt(q, k) * scale
        d = delta + jj * tq
        valid = (d > 0) if mode == "stick" else (d >= 0)
        if window is not None:
            valid = valid & (d < window)
        if has_sel:
            lane_blk = (off + lax.broadcasted_iota(jnp.int32, (LANES, tq), 1)) // NSA_CMP_BLOCK
            expand = (lax.broadcasted_iota(jnp.int32, (LANES, tq), 0) == lane_blk).astype(BF16)
            valid = valid & (jnp.dot(selm, expand, preferred_element_type=F32) > 0.5)
        if mode == "stick":
            valid = jnp.broadcast_to(valid[None], (nh, tq, tq)).reshape(nh * tq, tq)
            _stick_update(s, valid, v, tri, l_ref, acc_ref)
        else:
            s = s.reshape(nh, tq, tq)
            if has_bias:
                s = s + rb_ref[0] - cb_ref[0, :, j]
            _softmax_update(s, valid[None], v, m_ref, l_ref, acc_ref)
        return carry

    lax.fori_loop(0, i - lo + 1, body, 0)
    if mode == "stick":
        out = acc_ref[...]
    else:
        l = l_ref[...].reshape(nh * tq, 1)
        out = acc_ref[...] / jnp.where(l > 0.0, l, 1.0)
    for h in range(nh):
        o_ref[0, :, h * dv:(h + 1) * dv] = out[h * tq:(h + 1) * tq]


def _attn_prompt(q, kv, *, dq, dv, kcols, vcols, scale, mode="softmax", window=None, bias=None, sel=None, tq=128):
    b, t, _ = q.shape
    nh = N_HEADS_GROUP
    c = kv.shape[-1]
    in_specs = [pl.BlockSpec((1, tq, nh * dq), lambda bi, i: (bi, i, 0)),
                pl.BlockSpec((1, t, c), lambda bi, i: (bi, 0, 0))]
    args = [q, kv]
    if bias is not None:
        rb, cb = bias
        in_specs += [pl.BlockSpec((1, nh, tq, 1), lambda bi, i: (bi, 0, i, 0)),
                     pl.BlockSpec((1, nh, t // tq, 1, tq), lambda bi, i: (bi, 0, 0, 0, 0))]
        args += [rb, cb]
    if sel is not None:
        in_specs.append(pl.BlockSpec((1, tq, LANES), lambda bi, i: (bi, i, 0)))
        args.append(sel)
    body = functools.partial(_attn_prompt_body, tq=tq, dq=dq, dv=dv, kcols=kcols, vcols=vcols, scale=scale,
                             mode=mode, window=window, has_bias=bias is not None, has_sel=sel is not None)
    state = (nh * tq, 1) if mode == "stick" else (nh, tq, 1)
    return pl.pallas_call(
        body,
        grid=(b, t // tq),
        in_specs=in_specs,
        out_specs=pl.BlockSpec((1, tq, nh * dv), lambda bi, i: (bi, i, 0)),
        out_shape=jax.ShapeDtypeStruct((b, t, nh * dv), F32),
        scratch_shapes=[pltpu.VMEM((nh * tq, dv), F32), pltpu.VMEM(state, F32), pltpu.VMEM(state, F32)],
        compiler_params=_cparams("parallel", "arbitrary"),
        name="attn_prompt_" + mode + ("_b" if bias is not None else "") + ("_s" if sel is not None else "")
             + ("_w" if window is not None else ""),
    )(*args)


def _compress_prompt_body(x_ref, o_ref, *, rows):
    nb = rows // NSA_CMP_BLOCK
    pool = (lax.broadcasted_iota(jnp.int32, (nb, rows), 1) // NSA_CMP_BLOCK
            == lax.broadcasted_iota(jnp.int32, (nb, rows), 0)).astype(BF16)
    o_ref[0] = _pool_dot(pool, x_ref[0]) * (1.0 / NSA_CMP_BLOCK)


def _compress_prompt(x, rows=1024):
    b, t, c = x.shape
    return pl.pallas_call(
        functools.partial(_compress_prompt_body, rows=rows),
        grid=(b, t // rows),
        in_specs=[pl.BlockSpec((1, rows, c), lambda bi, i: (bi, i, 0))],
        out_specs=pl.BlockSpec((1, rows // NSA_CMP_BLOCK, c), lambda bi, i: (bi, i, 0)),
        out_shape=jax.ShapeDtypeStruct((b, t // NSA_CMP_BLOCK, c), F32),
        compiler_params=_cparams("parallel", "parallel"),
        name="compress_prompt",
    )(x)


def _topk_mask(score, lanes, k):
    lane = lax.broadcasted_iota(jnp.int32, score.shape, score.ndim - 1)
    rank = jnp.zeros(score.shape, F32)
    for c in lanes:
        col = score[:, c:c + 1]
        tie = jnp.where(lane > c, 1.0, 0.0)
        rank = rank + jnp.where(col > score, 1.0, jnp.where(col == score, tie, 0.0))
    return jnp.where(rank < k, 1.0, 0.0)


def _cmp_core(q, cmp, qpos, n_lanes, rows_per_head, k_sel):
    nh, r = N_HEADS_GROUP, rows_per_head
    lane = lax.broadcasted_iota(jnp.int32, (r, n_lanes), 1)
    s = _bdot_nt(q, cmp[:, :HEAD_DIM]) * (HEAD_DIM ** -0.5)
    s = s.reshape(nh, r, n_lanes)
    valid = (lane * NSA_CMP_BLOCK + (NSA_CMP_BLOCK - 1) <= qpos)[None]
    s = jnp.where(valid, s, NEG)
    m = jnp.max(s, axis=-1, keepdims=True)
    e = jnp.where(valid, jnp.exp(s - m), 0.0)
    den = jnp.sum(e, axis=-1, keepdims=True)
    p = e / jnp.where(den > 0.0, den, 1.0)
    o_c = _bdot(p.reshape(nh * r, n_lanes), cmp[:, HEAD_DIM:])
    imp = p[0] + p[1] + p[2] + p[3]
    imp = imp + pltpu.roll(imp, n_lanes - 1, 1)
    ratio = NSA_SEL_BLOCK // NSA_CMP_BLOCK
    blk = lane // ratio
    cur = qpos // NSA_SEL_BLOCK
    forced = (blk == 0) | (blk == cur) | (blk == cur - 1)
    future = blk * NSA_SEL_BLOCK > qpos
    score = jnp.where(forced, NSA_FORCED_SCORE, jnp.where(future, -1.0, imp))
    even = (lane % ratio) == 0
    score = jnp.where(even, score, -2.0)
    sel = _topk_mask(score, range(0, n_lanes, ratio), k_sel) * jnp.where(even, 1.0, 0.0)
    return o_c, sel + pltpu.roll(sel, 1, 1)


def _cmp_prompt_body(q_ref, cmp_ref, oc_ref, sel_ref, *, tq):
    nh = N_HEADS_GROUP
    i = pl.program_id(1)
    n_lanes = cmp_ref.shape[1]
    q = jnp.concatenate([q_ref[0, :, h * HEAD_DIM:(h + 1) * HEAD_DIM] for h in range(nh)], axis=0).astype(BF16)
    qpos = i * tq + lax.broadcasted_iota(jnp.int32, (tq, n_lanes), 0)
    o_c, sel = _cmp_core(q, cmp_ref[0], qpos, n_lanes, tq, NSA_TOPK)
    for h in range(nh):
        oc_ref[0, :, h * HEAD_DIM:(h + 1) * HEAD_DIM] = o_c[h * tq:(h + 1) * tq]
    sel_ref[0] = sel


def _cmp_prompt(q, cmp, tq=128):
    b, t, w = q.shape
    n_lanes = cmp.shape[1]
    return pl.pallas_call(
        functools.partial(_cmp_prompt_body, tq=tq),
        grid=(b, t // tq),
        in_specs=[pl.BlockSpec((1, tq, w), lambda bi, i: (bi, i, 0)),
                  pl.BlockSpec((1, n_lanes, 2 * HEAD_DIM), lambda bi, i: (bi, 0, 0))],
        out_specs=[pl.BlockSpec((1, tq, w), lambda bi, i: (bi, i, 0)),
                   pl.BlockSpec((1, tq, n_lanes), lambda bi, i: (bi, i, 0))],
        out_shape=[jax.ShapeDtypeStruct((b, t, w), F32), jax.ShapeDtypeStruct((b, t, n_lanes), F32)],
        compiler_params=_cparams("parallel", "parallel"),
        name="cmp_prompt",
    )(q, cmp)


def _cmp_decode_body(q_ref, cmp_ref, oc_ref, sel_ref, *, past_len, n_new_forced):
    n_lanes = cmp_ref.shape[1]
    qpos = past_len + lax.broadcasted_iota(jnp.int32, (DEC_ROWS, n_lanes), 0)
    o_c, sel = _cmp_core(q_ref[0].astype(BF16), cmp_ref[0], qpos, n_lanes, DEC_ROWS, NSA_TOPK - n_new_forced)
    oc_ref[0] = o_c
    sel_ref[0] = sel


def _cmp_decode(q, cmp, past_len):
    b, r, d = q.shape
    n_lanes = cmp.shape[1]
    return pl.pallas_call(
        functools.partial(_cmp_decode_body, past_len=past_len, n_new_forced=1),
        grid=(b,),
        in_specs=[pl.BlockSpec((1, r, d), lambda bi: (bi, 0, 0)),
                  pl.BlockSpec((1, n_lanes, 2 * HEAD_DIM), lambda bi: (bi, 0, 0))],
        out_specs=[pl.BlockSpec((1, r, d), lambda bi: (bi, 0, 0)),
                   pl.BlockSpec((1, DEC_ROWS, n_lanes), lambda bi: (bi, 0, 0))],
        out_shape=[jax.ShapeDtypeStruct((b, r, d), F32), jax.ShapeDtypeStruct((b, DEC_ROWS, n_lanes), F32)],
        compiler_params=_cparams("parallel"),
        name="cmp_decode",
    )(q, cmp)


def _page_specs(block, pps, nchunks, reverse):
    def spec(k):
        def index(bi, c, pt):
            chunk = (nchunks - 1 - c) if reverse else c
            return (pt[bi, chunk * pps + k],) + (0,) * (len(block) - 1)
        return pl.BlockSpec(block, index)
    return [spec(k) for k in range(pps)]


def _compress_decode_body(pt_ref, *refs, pps):
    del pt_ref
    o_ref = refs[pps]
    pool = (lax.broadcasted_iota(jnp.int32, (DEC_ROWS, PAGE_SIZE), 1) // NSA_CMP_BLOCK
            == lax.broadcasted_iota(jnp.int32, (DEC_ROWS, PAGE_SIZE), 0)).astype(BF16)
    for k in range(pps):
        o_ref[0, k] = _pool_dot(pool, refs[k][0]) * (1.0 / NSA_CMP_BLOCK)


def _compress_decode(pool, page_table, pps=16):
    b, n_pages = page_table.shape
    c = pool.shape[-1]
    nchunks = n_pages // pps
    grid_spec = pltpu.PrefetchScalarGridSpec(
        num_scalar_prefetch=1,
        grid=(b, nchunks),
        in_specs=_page_specs((1, PAGE_SIZE, c), pps, nchunks, False),
        out_specs=pl.BlockSpec((1, pps, DEC_ROWS, c), lambda bi, ci, pt: (bi, ci, 0, 0)),
    )
    return pl.pallas_call(
        functools.partial(_compress_decode_body, pps=pps),
        grid_spec=grid_spec,
        out_shape=jax.ShapeDtypeStruct((b, n_pages, DEC_ROWS, c), F32),
        compiler_params=_cparams("parallel", "parallel"),
        name="compress_decode",
    )(page_table, *([pool] * pps))


def _gather_rows8_body(pt_ref, *refs, pps):
    del pt_ref
    o_ref = refs[pps]
    for k in range(pps):
        o_ref[0, :, k * PAGE_SIZE:(k + 1) * PAGE_SIZE] = refs[k][0]


def _gather_rows8(pool_t, page_table, pps=16):
    b, n_pages = page_table.shape
    nchunks = n_pages // pps
    grid_spec = pltpu.PrefetchScalarGridSpec(
        num_scalar_prefetch=1,
        grid=(b, nchunks),
        in_specs=_page_specs((1, DEC_ROWS, PAGE_SIZE), pps, nchunks, False),
        out_specs=pl.BlockSpec((1, DEC_ROWS, pps * PAGE_SIZE), lambda bi, ci, pt: (bi, 0, ci)),
    )
    return pl.pallas_call(
        functools.partial(_gather_rows8_body, pps=pps),
        grid_spec=grid_spec,
        out_shape=jax.ShapeDtypeStruct((b, DEC_ROWS, n_pages * PAGE_SIZE), F32),
        compiler_params=_cparams("parallel", "parallel"),
        name="gather_logf",
    )(page_table, *([pool_t] * pps))


def _cumsum_body(x_ref, o_ref):
    x = x_ref[0]
    n = x.shape[-1]
    lane = lax.broadcasted_iota(jnp.int32, x.shape, 1)
    sh = 1
    while sh < n:
        x = x + jnp.where(lane >= sh, pltpu.roll(x, sh, 1), 0.0)
        sh *= 2
    o_ref[0] = x


def _cumsum_lanes(x):
    g, r, n = x.shape
    return pl.pallas_call(
        _cumsum_body,
        grid=(g,),
        in_specs=[pl.BlockSpec((1, r, n), lambda i: (i, 0, 0))],
        out_specs=pl.BlockSpec((1, r, n), lambda i: (i, 0, 0)),
        out_shape=jax.ShapeDtypeStruct((g, r, n), F32),
        compiler_params=_cparams("parallel"),
        name="cumsum",
    )(x)


def _attn_decode_body(pt_ref, q_ref, rb_ref, cb_ref, cbn_ref, new_ref, *refs, pps, kparts, dkp, vcols, scale, mode):
    del pt_ref
    pages = refs[:pps]
    o_ref = refs[pps]
    kb_ref, vb_ref, acc_ref, m_ref, l_ref = refs[pps + 1:]
    c = pl.program_id(1)
    q = q_ref[0].astype(BF16)
    rb = rb_ref[0]
    tri = _tri_later(PAGE_SIZE) if mode == "stick" else None

    def stage(page, slot):
        r0 = slot * PAGE_SIZE
        if dkp > sum(e - s for s, e, _ in kparts):
            kb_ref[r0:r0 + PAGE_SIZE, dkp - LANES:] = jnp.zeros((PAGE_SIZE, LANES), BF16)
        for s, e, d in kparts:
            kb_ref[r0:r0 + PAGE_SIZE, d:d + e - s] = page[0, :, s:e].astype(BF16)
        vb_ref[r0:r0 + PAGE_SIZE, :] = page[0, :, vcols[0]:vcols[1]].astype(BF16)

    def attend(slot0, nslots, cbias):
        if mode == "stick":
            for k in reversed(range(nslots)):
                r0 = (slot0 + k) * PAGE_SIZE
                cbk = cbias[:, k * PAGE_SIZE:(k + 1) * PAGE_SIZE]
                z = _bdot_nt(q, kb_ref[r0:r0 + PAGE_SIZE, :]) * scale
                _stick_update(z, cbk > MASK_CUT, vb_ref[r0:r0 + PAGE_SIZE, :], tri, l_ref, acc_ref)
        else:
            r0, r1 = slot0 * PAGE_SIZE, (slot0 + nslots) * PAGE_SIZE
            s = _bdot_nt(q, kb_ref[r0:r1, :]) * scale + rb + cbias
            _softmax_update(s, cbias > MASK_CUT, vb_ref[r0:r1, :], m_ref, l_ref, acc_ref)

    @pl.when(c == 0)
    def _():
        acc_ref[...] = jnp.zeros_like(acc_ref)
        l_ref[...] = jnp.zeros_like(l_ref)
        m_ref[...] = jnp.full_like(m_ref, NEG)
        stage(new_ref, pps)
        attend(pps, 1, cbn_ref[0])

    for k in range(pps):
        stage(pages[k], k)
    attend(0, pps, cb_ref[0])

    @pl.when(c == pl.num_programs(1) - 1)
    def _():
        if mode == "stick":
            o_ref[0] = acc_ref[...]
        else:
            l = l_ref[...]
            o_ref[0] = acc_ref[...] / jnp.where(l > 0.0, l, 1.0)


def _attn_decode(q, pool, page_table, new_page, rb, cb, *, kparts, dkp, vcols, scale, mode="softmax", pps=16):
    b, r, dq = q.shape
    n_pages = page_table.shape[1]
    c = pool.shape[-1]
    nchunks = n_pages // pps
    dv = vcols[1] - vcols[0]
    rbb = (lambda bi: bi) if rb.shape[0] > 1 else (lambda bi: 0)
    cbb = (lambda bi: bi) if cb.shape[0] > 1 else (lambda bi: 0)
    in_specs = [
        pl.BlockSpec((1, r, dq), lambda bi, ci, pt: (bi, 0, 0)),
        pl.BlockSpec((1, r, 1), lambda bi, ci, pt: (rbb(bi), 0, 0)),
        pl.BlockSpec((1, r, pps * PAGE_SIZE), lambda bi, ci, pt: (cbb(bi), 0, nchunks - 1 - ci)),
        pl.BlockSpec((1, r, PAGE_SIZE), lambda bi, ci, pt: (cbb(bi), 0, n_pages)),
        pl.BlockSpec((1, PAGE_SIZE, c), lambda bi, ci, pt: (bi, 0, 0)),
    ] + _page_specs((1, PAGE_SIZE, c), pps, nchunks, True)
    grid_spec = pltpu.PrefetchScalarGridSpec(
        num_scalar_prefetch=1,
        grid=(b, nchunks),
        in_specs=in_specs,
        out_specs=pl.BlockSpec((1, r, dv), lambda bi, ci, pt: (bi, 0, 0)),
        scratch_shapes=[pltpu.VMEM(((pps + 1) * PAGE_SIZE, dkp), BF16),
                        pltpu.VMEM(((pps + 1) * PAGE_SIZE, dv), BF16),
                        pltpu.VMEM((r, dv), F32), pltpu.VMEM((r, 1), F32), pltpu.VMEM((r, 1), F32)],
    )
    body = functools.partial(_attn_decode_body, pps=pps, kparts=kparts, dkp=dkp, vcols=vcols, scale=scale, mode=mode)
    return pl.pallas_call(
        body,
        grid_spec=grid_spec,
        out_shape=jax.ShapeDtypeStruct((b, r, dv), F32),
        compiler_params=_cparams("parallel", "arbitrary"),
        name="attn_decode_" + mode + ("_mla" if dkp > LANES else ""),
    )(page_table, q, rb, cb, cb, new_page, *([pool] * pps))


_IN_SIZES = (MLA_Q_RANK, MLA_KV_RANK, MLA_ROPE, GROUP_WIDTH, 2 * HEAD_DIM, 2 * HEAD_DIM, 2 * HEAD_DIM,
             3 * N_HEADS_GROUP, GROUP_WIDTH, 2 * HEAD_DIM, GROUP_WIDTH, 2 * HEAD_DIM, N_HEADS_GROUP)
_GATE_LANES = 3 * N_HEADS_GROUP
FFN_TILE = 512


def _tile(n, pref):
    for t in range(min(pref, n), 7, -1):
        if n % t == 0 and t % 8 == 0:
            return t
    raise ValueError(f"no tile for {n}")


def _rope_tables(pos):
    posf = pos.astype(F32)[:, None]

    def cos_sin(half):
        inv = ROPE_THETA ** (-jnp.arange(half, dtype=F32) / half)
        ang = posf * inv[None, :]
        return jnp.cos(ang), jnp.sin(ang)

    c, s = cos_sin(HEAD_DIM // 2)
    c1, s1 = jnp.concatenate([c, c], axis=1), jnp.concatenate([-s, s], axis=1)
    c, s = cos_sin(MLA_ROPE // 2)
    z = jnp.zeros_like(s)
    c2 = jnp.concatenate([c, c, z, z], axis=1)
    s2a = jnp.concatenate([-s, z, z, z], axis=1)
    s2b = jnp.concatenate([z, s, z, z], axis=1)
    return c1, s1, c2, s2a, s2b


def _relayout_w_in(w):
    d = w.shape[0]
    parts, start = [], 0
    for size in _IN_SIZES:
        parts.append(w[:, start:start + size])
        start += size
    dq, dkv, kr, nq, ncmp, nsel, nwin, ngate, sq, skv, fq, fkv, fg = parts
    zeros = lambda n: jnp.zeros((d, n), w.dtype)
    out = jnp.concatenate([dq, dkv, kr, zeros(LANES - MLA_ROPE), nq, ncmp, nsel, nwin, sq, skv, fq, fkv,
                           ngate, fg, zeros(LANES - _GATE_LANES - N_HEADS_GROUP)], axis=1)
    assert out.shape[1] == N_IN_PAD
    return out.astype(BF16)


def _pad_to(a, axis, n):
    pad = [(0, 0)] * a.ndim
    pad[axis] = (0, n - a.shape[axis])
    return jnp.pad(a, pad)


def _gain_rows(rows, width=MLA_Q_RANK, n_rows=16):
    out = jnp.stack([_pad_to(r, 0, width) for r in rows], axis=0)
    return _pad_to(out, 0, n_rows)


def kernel(x_prompt, x_sample, cache_mla, cache_nsa_cmp, cache_nsa_sel, state_nsa_win, cache_sb, cache_fox_kv,
           cache_fox_logf, page_table, attn_norm, ffn_norm, ffn_w_in, ffn_w_out, w_in, w_out, mla_q_norm, mla_w_uq,
           mla_qk_norm, mla_kv_norm, mla_kr_norm, mla_w_uk, mla_w_uv, nsa_q_norm, nsa_k_norm, fox_q_norm, fox_k_norm,
           fox_f_bias):
    nb, t, d = x_prompt.shape
    db, ts, _ = x_sample.shape
    depth = w_in.shape[0]
    n_pages = page_table.shape[1]
    past = n_pages * PAGE_SIZE
    n_p, n_s = nb * t, db * ts
    nh = N_HEADS_GROUP
    assert ts <= DEC_ROWS and ts <= NSA_SEL_BLOCK and past % NSA_SEL_BLOCK == 0 and t % LANES == 0
    wb = state_nsa_win.shape[2]
    assert wb == NSA_WINDOW and wb % PAGE_SIZE == 0
    d_ff = ffn_w_out.shape[2]
    ffp = -(-d_ff // FFN_TILE) * FFN_TILE
    page_table = page_table.astype(jnp.int32)

    x = jnp.concatenate([x_prompt.reshape(n_p, d), x_sample.reshape(n_s, d)], axis=0)
    n_t = n_p + n_s
    pos = jnp.concatenate([jnp.tile(jnp.arange(t, dtype=jnp.int32), nb),
                           jnp.tile(past + jnp.arange(ts, dtype=jnp.int32), db)])
    tabs = _rope_tables(pos)
    tm_ffn, tm_proj, tq = _tile(n_t, 512), _tile(n_t, 256), LANES

    n_k = (n_pages + 1) * PAGE_SIZE
    kpos = jnp.arange(n_k, dtype=jnp.int32)[None, :]
    qpos8 = past + jnp.arange(DEC_ROWS, dtype=jnp.int32)[:, None]

    def rows32(a):
        return jnp.broadcast_to(a[..., None, :, :], a.shape[:-2] + (nh,) + a.shape[-2:]).reshape(
            a.shape[:-2] + (nh * DEC_ROWS, a.shape[-1]))

    causal8 = kpos <= qpos8
    strict8 = kpos < qpos8
    as_bias = lambda m: jnp.where(m, 0.0, NEG).astype(F32)
    cb_causal = rows32(as_bias(causal8))[None]
    cb_strict = rows32(as_bias(strict8))[None]
    wpos = (past - wb + jnp.arange(wb + PAGE_SIZE, dtype=jnp.int32))[None, :]
    wdiff = qpos8 - wpos
    cb_window = rows32(as_bias((wdiff >= 0) & (wdiff < NSA_WINDOW)))[None]
    rb_zero = jnp.zeros((1, nh * DEC_ROWS, 1), F32)
    win_table = jnp.arange(db * (wb // PAGE_SIZE), dtype=jnp.int32).reshape(db, wb // PAGE_SIZE)

    def dec_q(a, dq):
        a = a[n_p:].reshape(db, ts, nh, dq).transpose(0, 2, 1, 3)
        return _pad_to(a, 2, DEC_ROWS).reshape(db, nh * DEC_ROWS, dq)

    def dec_out(o):
        dv = o.shape[-1]
        return o.reshape(db, nh, DEC_ROWS, dv)[:, :, :ts].transpose(0, 2, 1, 3).reshape(n_s, nh * dv)

    def prompt(a):
        return a[:n_p].reshape(nb, t, a.shape[-1])

    def sample(a):
        return a[n_p:].reshape(db, ts, a.shape[-1])

    def new_page(a):
        return _pad_to(sample(a), 1, PAGE_SIZE)

    st_p = [[] for _ in range(7)]
    st_s = [[] for _ in range(7)]
    scale = HEAD_DIM ** -0.5
    kv_std = dict(kparts=((0, HEAD_DIM, 0),), dkp=HEAD_DIM, vcols=(HEAD_DIM, 2 * HEAD_DIM), scale=scale)
    for l in range(depth):
        def ffn(x, i):
            wa = _pad_to(ffn_w_in[l, i][:, :d_ff], 1, ffp).astype(BF16)
            wb_ = _pad_to(ffn_w_in[l, i][:, d_ff:], 1, ffp).astype(BF16)
            wo = _pad_to(ffn_w_out[l, i], 0, ffp).astype(BF16)
            return _ffn(x, ffn_norm[l, i][None, :], wa, wb_, wo, tm_ffn, FFN_TILE)

        x = ffn(x, 0)
        wuq = _pad_to(mla_w_uq[l].reshape(MLA_Q_RANK, nh, MLA_QK), 2, MLA_PAD).reshape(MLA_Q_RANK, nh * MLA_PAD)
        vec = _gain_rows([mla_q_norm[l], mla_qk_norm[l, :MLA_NOPE], mla_qk_norm[l, MLA_NOPE:], mla_kv_norm[l],
                          mla_kr_norm[l], nsa_q_norm[l], nsa_k_norm[l, 0], nsa_k_norm[l, 1], nsa_k_norm[l, 2],
                          fox_q_norm[l], fox_k_norm[l]])
        fb = jnp.zeros((1, LANES), F32).at[0, _GATE_LANES:_GATE_LANES + nh].set(fox_f_bias[l])
        (mlaq, mlarow, nsaq, cmprow, selrow, winrow, sbq, sbrow, foxq, foxrow, gate, logf_t) = _proj(
            x, attn_norm[l][None, :], _relayout_w_in(w_in[l]), wuq.astype(BF16),
            jnp.transpose(mla_w_uk[l], (0, 2, 1)).astype(BF16), vec, fb, tabs, tm_proj)
        mlarow = mlarow[:, :MLA_QK]
        logf = logf_t[:, _GATE_LANES:_GATE_LANES + nh]

        olat_p = _attn_prompt(prompt(mlaq), _pad_to(prompt(mlarow), 2, MLA_PAD), dq=MLA_PAD, dv=MLA_KV_RANK,
                              kcols=(0, MLA_PAD), vcols=(0, MLA_KV_RANK), scale=MLA_QK ** -0.5, tq=tq)
        std = dict(dq=HEAD_DIM, dv=HEAD_DIM, kcols=(0, HEAD_DIM), vcols=(HEAD_DIM, 2 * HEAD_DIM), scale=scale, tq=tq)
        cmp_p = _compress_prompt(prompt(cmprow), rows=min(1024, t))
        oc_p, selm_p = _cmp_prompt(prompt(nsaq), cmp_p, tq=tq)
        os_p = _attn_prompt(prompt(nsaq), prompt(selrow), sel=selm_p, **std)
        ow_p = _attn_prompt(prompt(nsaq), prompt(winrow), window=NSA_WINDOW, **std)
        osb_p = _attn_prompt(prompt(sbq), prompt(sbrow), mode="stick", **std)
        lf_p = _pad_to(jnp.transpose(prompt(logf), (0, 2, 1)), 1, DEC_ROWS)
        cum_p = _cumsum_lanes(lf_p)[:, :nh]
        ofox_p = _attn_prompt(prompt(foxq), prompt(foxrow),
                              bias=(cum_p[..., None], cum_p.reshape(nb, nh, t // tq, 1, tq)), **std)

        olat_s = _attn_decode(dec_q(mlaq, MLA_PAD), cache_mla[l], page_table, new_page(mlarow), rb_zero, cb_causal,
                              kparts=((0, MLA_KV_RANK, 0), (MLA_KV_RANK, MLA_QK, MLA_KV_RANK)), dkp=MLA_PAD,
                              vcols=(0, MLA_KV_RANK), scale=MLA_QK ** -0.5)
        nsaq_s = dec_q(nsaq, HEAD_DIM)
        cmp_s = _compress_decode(cache_nsa_cmp[l], page_table)[:, :, :PAGE_SIZE // NSA_CMP_BLOCK]
        cmp_s = cmp_s.reshape(db, past // NSA_CMP_BLOCK, 2 * HEAD_DIM)
        oc_s, selm_s = _cmp_decode(nsaq_s, cmp_s, past)
        sel_keys = jnp.repeat(selm_s, NSA_CMP_BLOCK, axis=-1) > 0.5
        sel_keys = jnp.concatenate([sel_keys, jnp.ones((db, DEC_ROWS, PAGE_SIZE), bool)], axis=-1)
        os_s = _attn_decode(nsaq_s, cache_nsa_sel[l], page_table, new_page(selrow), rb_zero,
                            rows32(as_bias(sel_keys & causal8[None])), **kv_std)
        ow_s = _attn_decode(nsaq_s, state_nsa_win[l].reshape(db * (wb // PAGE_SIZE), PAGE_SIZE, 2 * HEAD_DIM),
                            win_table, new_page(winrow), rb_zero, cb_window, pps=wb // PAGE_SIZE, **kv_std)
        osb_s = _attn_decode(dec_q(sbq, HEAD_DIM), cache_sb[l], page_table, new_page(sbrow), rb_zero, cb_strict,
                             mode="stick", **kv_std)
        lf_pool = _pad_to(jnp.transpose(cache_fox_logf[l], (0, 2, 1)), 1, DEC_ROWS)
        lf_new = _pad_to(_pad_to(jnp.transpose(sample(logf), (0, 2, 1)), 1, DEC_ROWS), 2, PAGE_SIZE)
        cum_s = _cumsum_lanes(jnp.concatenate([_gather_rows8(lf_pool, page_table), lf_new], axis=-1))[:, :nh]
        cq = _pad_to(cum_s[:, :, past:past + ts], 2, DEC_ROWS).reshape(db, nh * DEC_ROWS, 1)
        ck = jnp.broadcast_to(cum_s[:, :, None, :], (db, nh, DEC_ROWS, n_k))
        cb_fox = jnp.where(causal8[None, None], -ck, NEG).reshape(db, nh * DEC_ROWS, n_k)
        ofox_s = _attn_decode(dec_q(foxq, HEAD_DIM), cache_fox_kv[l], page_table, new_page(foxrow), cq, cb_fox,
                              **kv_std)

        both = lambda p, s: jnp.concatenate([p.reshape(n_p, p.shape[-1]), dec_out(s)], axis=0)
        x = _outproj(x, both(olat_p, olat_s), both(oc_p, oc_s), both(os_p, os_s), both(ow_p, ow_s), gate,
                     both(osb_p, osb_s), both(ofox_p, ofox_s), mla_w_uv[l].astype(BF16), w_out[l].astype(BF16),
                     tm_ffn)
        x = ffn(x, 1)

        wp = min(NSA_WINDOW, t)
        win_s = jnp.concatenate([state_nsa_win[l], sample(winrow)], axis=1)[:, -wb:]
        for lst, a in zip(st_p, (prompt(mlarow), prompt(cmprow), prompt(selrow), prompt(winrow)[:, -wp:],
                                 prompt(sbrow), prompt(foxrow), prompt(logf))):
            lst.append(a)
        for lst, a in zip(st_s, (sample(mlarow), sample(cmprow), sample(selrow), win_s,
                                 sample(sbrow), sample(foxrow), sample(logf))):
            lst.append(a)

    outs_p = [jnp.stack(s, axis=0) for s in st_p]
    outs_s = [jnp.stack(s, axis=0) for s in st_s]
    res = [x[:n_p].reshape(nb, t, d), x[n_p:].reshape(db, ts, d)]
    for p, s in zip(outs_p, outs_s):
        res += [p, s]
    return tuple(res)
```

```python
import functools

import jax
import jax.numpy as jnp
import numpy as np
from jax import lax
from jax.experimental import pallas as pl
from jax.experimental.pallas import tpu as pltpu

F32 = jnp.float32
BF16 = jnp.bfloat16

HEAD_DIM = 128
N_HEADS_GROUP = 4
GROUP_WIDTH = N_HEADS_GROUP * HEAD_DIM
MLA_Q_RANK = 384
MLA_KV_RANK = 128
MLA_NOPE = 128
MLA_ROPE = 64
MLA_QK = MLA_NOPE + MLA_ROPE
MLA_PAD = 256
NSA_CMP_BLOCK = 32
NSA_SEL_BLOCK = 64
NSA_TOPK = 16
NSA_WINDOW = 512
NSA_FORCED_SCORE = 1.0e4
ROPE_THETA = 10000.0
RMS_EPS = 1e-6
PAGE_SIZE = 128
NEG = -1.0e30
MASK_CUT = -1.0e29

LANES = 128
DEC_ROWS = 8
VMEM_LIMIT = 56 * 1024 * 1024


def _cparams(*sem):
    return pltpu.CompilerParams(dimension_semantics=sem, vmem_limit_bytes=VMEM_LIMIT)


def _bdot(a, b):
    return jnp.dot(a.astype(BF16), b.astype(BF16), preferred_element_type=F32)


def _bdot_nt(a, b):
    return lax.dot_general(a.astype(BF16), b.astype(BF16), (((1,), (1,)), ((), ())),
                           preferred_element_type=F32)


def _split_dot(a, b01):
    hi = a.astype(BF16)
    r1 = a - hi.astype(F32)
    mid = r1.astype(BF16)
    lo = (r1 - mid.astype(F32)).astype(BF16)
    dot = functools.partial(jnp.dot, preferred_element_type=F32)
    return dot(hi, b01) + dot(mid, b01) + dot(lo, b01)


def _rms(x, g, n=None):
    n = x.shape[-1] if n is None else n
    ms = jnp.sum(x * x, axis=-1, keepdims=True) * (1.0 / n)
    return x * lax.rsqrt(ms + RMS_EPS) * g


def _log_sigmoid(x):
    return jnp.minimum(x, 0.0) - jnp.log1p(jnp.exp(-jnp.abs(x)))


def _ffn_body(x_ref, g_ref, wa_ref, wb_ref, wo_ref, o_ref, xn_ref, acc_ref):
    j = pl.program_id(1)

    @pl.when(j == 0)
    def _():
        xn_ref[...] = _rms(x_ref[...], g_ref[...]).astype(BF16)
        acc_ref[...] = jnp.zeros_like(acc_ref)

    xn = xn_ref[...]
    a = jnp.dot(xn, wa_ref[...], preferred_element_type=F32)
    b = jnp.dot(xn, wb_ref[...], preferred_element_type=F32)
    act = (a * jax.nn.sigmoid(a) * b).astype(BF16)
    acc_ref[...] += jnp.dot(act, wo_ref[...], preferred_element_type=F32)

    @pl.when(j == pl.num_programs(1) - 1)
    def _():
        o_ref[...] = x_ref[...] + 0.5 * acc_ref[...]


def _ffn(x, g, wa, wb, wo, tm, tf):
    t, d = x.shape
    ff = wa.shape[1]
    return pl.pallas_call(
        _ffn_body,
        grid=(t // tm, ff // tf),
        in_specs=[
            pl.BlockSpec((tm, d), lambda i, j: (i, 0)),
            pl.BlockSpec((1, d), lambda i, j: (0, 0)),
            pl.BlockSpec((d, tf), lambda i, j: (0, j)),
            pl.BlockSpec((d, tf), lambda i, j: (0, j)),
            pl.BlockSpec((tf, d), lambda i, j: (j, 0)),
        ],
        out_specs=pl.BlockSpec((tm, d), lambda i, j: (i, 0)),
        out_shape=jax.ShapeDtypeStruct((t, d), F32),
        scratch_shapes=[pltpu.VMEM((tm, d), BF16), pltpu.VMEM((tm, d), F32)],
        compiler_params=_cparams("parallel", "arbitrary"),
        name="ffn",
    )(x, g, wa, wb, wo)


_C_DQ = 0
_C_DKV = 384
_C_KR = 512
_C_NQ = 640
_C_NCMP = 1152
_C_NSEL = 1408
_C_NWIN = 1664
_C_SQ = 1920
_C_SKV = 2432
_C_FQ = 2688
_C_FKV = 3200
_C_GATE = 3456
N_IN_PAD = 3584


def _rope128(x, c, s):
    return x * c + pltpu.roll(x, 64, 1) * s


def _rope64(x, c, sa, sb):
    return x * c + pltpu.roll(x, 96, 1) * sa + pltpu.roll(x, 32, 1) * sb


def _proj_body(x_ref, gin_ref, win_ref, wuq_ref, wuk_ref, vec_ref, fb_ref,
               c1_ref, s1_ref, c2_ref, s2a_ref, s2b_ref,
               mlaq_ref, mlarow_ref, nsaq_ref, cmp_ref, sel_ref, win_ref_o, sbq_ref, sbrow_ref,
               foxq_ref, foxrow_ref, gate_ref, logf_ref):
    xn = _rms(x_ref[...], gin_ref[...]).astype(BF16)
    z = jnp.dot(xn, win_ref[...], preferred_element_type=F32)
    c1, s1 = c1_ref[...], s1_ref[...]
    c2, s2a, s2b = c2_ref[...], s2a_ref[...], s2b_ref[...]
    vec = vec_ref[...]

    def gain(row, width=HEAD_DIM):
        return vec[row:row + 1, :width]

    dqn = _rms(z[:, _C_DQ:_C_DQ + MLA_Q_RANK], gain(0, MLA_Q_RANK))
    qh = _bdot(dqn, wuq_ref[...])
    for h in range(N_HEADS_GROUP):
        a = qh[:, h * MLA_PAD:h * MLA_PAD + MLA_NOPE]
        b = qh[:, h * MLA_PAD + MLA_NOPE:(h + 1) * MLA_PAD]
        ms = (jnp.sum(a * a, axis=-1, keepdims=True) + jnp.sum(b * b, axis=-1, keepdims=True)) * (1.0 / MLA_QK)
        rs = lax.rsqrt(ms + RMS_EPS)
        mlaq_ref[:, h * MLA_PAD:h * MLA_PAD + MLA_NOPE] = _bdot(a * rs * gain(1), wuk_ref[h])
        mlaq_ref[:, h * MLA_PAD + MLA_NOPE:(h + 1) * MLA_PAD] = _rope64(b * rs * gain(2), c2, s2a, s2b)
    mlarow_ref[:, :MLA_KV_RANK] = _rms(z[:, _C_DKV:_C_DKV + MLA_KV_RANK], gain(3))
    mlarow_ref[:, MLA_KV_RANK:] = _rope64(_rms(z[:, _C_KR:_C_KR + LANES], gain(4), MLA_ROPE), c2, s2a, s2b)
    for h in range(N_HEADS_GROUP):
        sl = slice(h * HEAD_DIM, (h + 1) * HEAD_DIM)
        nsaq_ref[:, sl] = _rope128(_rms(z[:, _C_NQ + h * HEAD_DIM:_C_NQ + (h + 1) * HEAD_DIM], gain(5)), c1, s1)
        sbq_ref[:, sl] = z[:, _C_SQ + h * HEAD_DIM:_C_SQ + (h + 1) * HEAD_DIM]
        foxq_ref[:, sl] = _rms(z[:, _C_FQ + h * HEAD_DIM:_C_FQ + (h + 1) * HEAD_DIM], gain(9))
    for i, (col, ref) in enumerate(((_C_NCMP, cmp_ref), (_C_NSEL, sel_ref), (_C_NWIN, win_ref_o))):
        ref[:, :HEAD_DIM] = _rope128(_rms(z[:, col:col + HEAD_DIM], gain(6 + i)), c1, s1)
        ref[:, HEAD_DIM:] = z[:, col + HEAD_DIM:col + 2 * HEAD_DIM]
    sbrow_ref[...] = z[:, _C_SKV:_C_SKV + 2 * HEAD_DIM]
    foxrow_ref[:, :HEAD_DIM] = _rms(z[:, _C_FKV:_C_FKV + HEAD_DIM], gain(10))
    foxrow_ref[:, HEAD_DIM:] = z[:, _C_FKV + HEAD_DIM:_C_FKV + 2 * HEAD_DIM]
    gt = z[:, _C_GATE:_C_GATE + LANES]
    gate_ref[...] = jax.nn.sigmoid(gt)
    logf_ref[...] = _log_sigmoid(gt + fb_ref[...])


_PROJ_OUT_WIDTHS = (4 * MLA_PAD, MLA_PAD, GROUP_WIDTH, 256, 256, 256, GROUP_WIDTH, 256, GROUP_WIDTH, 256,
                    LANES, LANES)


def _proj(x, gin, win, wuq, wuk, vec, fb, tabs, tm):
    t, d = x.shape
    row = lambda w: pl.BlockSpec((tm, w), lambda i: (i, 0))
    full = lambda a: pl.BlockSpec(a.shape, lambda i: (0,) * a.ndim)
    return pl.pallas_call(
        _proj_body,
        grid=(t // tm,),
        in_specs=[row(d), full(gin), full(win), full(wuq), full(wuk), full(vec), full(fb)]
                 + [row(LANES)] * 5,
        out_specs=[row(w) for w in _PROJ_OUT_WIDTHS],
        out_shape=[jax.ShapeDtypeStruct((t, w), F32) for w in _PROJ_OUT_WIDTHS],
        compiler_params=_cparams("parallel"),
        name="proj",
    )(x, gin, win, wuq, wuk, vec, fb, *tabs)


def _outproj_body(h_ref, olat_ref, oc_ref, os_ref, ow_ref, gate_ref, osb_ref, ofox_ref, wuv_ref, wout_ref, o_ref):
    acc = h_ref[...]
    gate = gate_ref[...]
    olat, oc, os_, ow = olat_ref[...], oc_ref[...], os_ref[...], ow_ref[...]
    mla, nsa = [], []
    for h in range(N_HEADS_GROUP):
        sl = slice(h * HEAD_DIM, (h + 1) * HEAD_DIM)
        mla.append(_bdot(olat[:, sl], wuv_ref[h]))
        nsa.append(gate[:, 3 * h:3 * h + 1] * oc[:, sl] + gate[:, 3 * h + 1:3 * h + 2] * os_[:, sl]
                   + gate[:, 3 * h + 2:3 * h + 3] * ow[:, sl])
    groups = (jnp.concatenate(mla, axis=1), jnp.concatenate(nsa, axis=1), osb_ref[...], ofox_ref[...])
    for g, og in enumerate(groups):
        acc = acc + _bdot(og, wout_ref[g * GROUP_WIDTH:(g + 1) * GROUP_WIDTH, :])
    o_ref[...] = acc


def _outproj(h, olat, oc, os_, ow, gate, osb, ofox, wuv, wout, tm):
    t, d = h.shape
    row = lambda w: pl.BlockSpec((tm, w), lambda i: (i, 0))
    full = lambda a: pl.BlockSpec(a.shape, lambda i: (0,) * a.ndim)
    return pl.pallas_call(
        _outproj_body,
        grid=(t // tm,),
        in_specs=[row(d)] + [row(GROUP_WIDTH)] * 4 + [row(LANES)] + [row(GROUP_WIDTH)] * 2 + [full(wuv), full(wout)],
        out_specs=row(d),
        out_shape=jax.ShapeDtypeStruct((t, d), F32),
        compiler_params=_cparams("parallel"),
        name="outproj",
    )(h, olat, oc, os_, ow, gate, osb, ofox, wuv, wout)


def _split2_dot(a, b01):
    hi = a.astype(BF16)
    lo = (a - hi.astype(F32)).astype(BF16)
    return (jnp.dot(hi, b01, preferred_element_type=F32) + jnp.dot(lo, b01, preferred_element_type=F32))


def _pool_dot(p01, x):
    hi = x.astype(BF16)
    lo = (x - hi.astype(F32)).astype(BF16)
    return (jnp.dot(p01, hi, preferred_element_type=F32) + jnp.dot(p01, lo, preferred_element_type=F32))


def _softplus(z):
    return jnp.maximum(z, 0.0) + jnp.log1p(jnp.exp(-jnp.abs(z)))


def _softmax_update(s, valid, pv, m_ref, l_ref, acc_ref):
    n = s.shape[-1]
    if valid is not None:
        s = jnp.where(valid, s, NEG)
    m_prev = m_ref[...]
    m_new = jnp.maximum(m_prev, jnp.max(s, axis=-1, keepdims=True))
    alpha = jnp.exp(m_prev - m_new)
    p = jnp.exp(s - m_new)
    if valid is not None:
        p = jnp.where(valid, p, 0.0)
    l_ref[...] = alpha * l_ref[...] + jnp.sum(p, axis=-1, keepdims=True)
    rows = acc_ref.shape[0]
    acc_ref[...] = acc_ref[...] * alpha.reshape(rows, 1) + pv(p.reshape(rows, n).astype(BF16))
    m_ref[...] = m_new


def _tri_later(n):
    r = lax.broadcasted_iota(jnp.int32, (n, n), 0)
    c = lax.broadcasted_iota(jnp.int32, (n, n), 1)
    return (r > c).astype(BF16)


def _stick_weights(z, valid, tri, carry_ref, width):
    n = z.shape[-1]
    sp = _softplus(z)
    lf = -sp if valid is None else jnp.where(valid, -sp, 0.0)
    pieces = [lf[:, k * width:(k + 1) * width] for k in range(n // width)]
    run = carry_ref[...]
    between = [None] * len(pieces)
    for k in reversed(range(len(pieces))):
        between[k] = _split2_dot(pieces[k], tri) + run
        run = run + jnp.sum(pieces[k], axis=-1, keepdims=True)
    carry_ref[...] = run
    between = between[0] if len(between) == 1 else jnp.concatenate(between, axis=1)
    a = jnp.exp(z - sp + between)
    return a if valid is None else jnp.where(valid, a, 0.0)


def _attn_prompt_body(*refs, tq, tk, dq, dv, kcols, vcols, scale, mode, window, has_bias, has_sel):
    it = iter(refs)
    q_ref, kv_ref = next(it), next(it)
    rb_ref = next(it) if has_bias else None
    cb_ref = next(it) if has_bias else None
    sel_ref = next(it) if has_sel else None
    o_ref = next(it)
    acc_ref, m_ref, l_ref = next(it), next(it), next(it)
    nh = N_HEADS_GROUP
    strict = mode == "stick"
    q0 = pl.program_id(1) * tq
    q = jnp.concatenate([q_ref[0, :, h * dq:(h + 1) * dq] for h in range(nh)], axis=0)
    q = (q * scale).astype(BF16)
    acc_ref[...] = jnp.zeros_like(acc_ref)
    l_ref[...] = jnp.zeros_like(l_ref)
    m_ref[...] = jnp.full_like(m_ref, NEG)
    delta = lax.broadcasted_iota(jnp.int32, (tq, tk), 0) - lax.broadcasted_iota(jnp.int32, (tq, tk), 1)
    selm = sel_ref[0].astype(BF16) if has_sel else None
    tri = _tri_later(tk) if strict else None
    n_hi = (q0 + tq - 1) // tk + 1
    lo = 0 if window is None else jnp.maximum(q0 - (window - 1), 0) // tk
    all_masked = has_sel or window is not None
    n_full = lo if all_masked else (q0 + (0 if strict else 1)) // tk

    def block(j, masked):
        off = pl.multiple_of(j * tk, tk)
        kv = kv_ref[0, pl.ds(off, tk), :]
        v = kv[:, vcols[0]:vcols[1]].astype(BF16)
        s = _bdot_nt(q, kv[:, kcols[0]:kcols[1]])
        valid = None
        if masked:
            d = delta + (q0 - off)
            valid = (d > 0) if strict else (d >= 0)
            if window is not None:
                valid = valid & (d < window)
            if has_sel:
                lane_blk = (off + lax.broadcasted_iota(jnp.int32, (LANES, tk), 1)) // NSA_CMP_BLOCK
                expand = (lax.broadcasted_iota(jnp.int32, (LANES, tk), 0) == lane_blk).astype(BF16)
                valid = valid & (jnp.dot(selm, expand, preferred_element_type=F32) > 0.5)
        if strict:
            if masked:
                valid = jnp.broadcast_to(valid[None], (nh, tq, tk)).reshape(nh * tq, tk)
            a = _stick_weights(s, valid, tri, l_ref, tk)
            acc_ref[...] += jnp.dot(a.astype(BF16), v, preferred_element_type=F32)
        else:
            s = s.reshape(nh, tq, tk)
            if has_bias:
                s = s + rb_ref[0] - cb_ref[0, :, j]
            _softmax_update(s, valid[None] if masked else None,
                            lambda p: jnp.dot(p, v, preferred_element_type=F32), m_ref, l_ref, acc_ref)

    def masked_step(jj, carry):
        block(n_hi - 1 - jj, True)
        return carry

    def full_step(jj, carry):
        block(n_full - 1 - jj, False)
        return carry

    lax.fori_loop(0, n_hi - n_full, masked_step, 0)
    if not all_masked:
        lax.fori_loop(0, n_full - lo, full_step, 0)
    if strict:
        out = acc_ref[...]
    else:
        l = l_ref[...].reshape(nh * tq, 1)
        out = acc_ref[...] / jnp.where(l > 0.0, l, 1.0)
    for h in range(nh):
        o_ref[0, :, h * dv:(h + 1) * dv] = out[h * tq:(h + 1) * tq]


def _attn_prompt(q, kv, *, dq, dv, kcols, vcols, scale, mode="softmax", window=None, bias=None, sel=None,
                 tq=128, tk=512):
    b, t, _ = q.shape
    nh = N_HEADS_GROUP
    c = kv.shape[-1]
    in_specs = [pl.BlockSpec((1, tq, nh * dq), lambda bi, i: (bi, i, 0)),
                pl.BlockSpec((1, t, c), lambda bi, i: (bi, 0, 0))]
    args = [q, kv]
    if bias is not None:
        in_specs += [pl.BlockSpec((1, nh, tq, 1), lambda bi, i: (bi, 0, i, 0)),
                     pl.BlockSpec((1, nh, t // tk, 1, tk), lambda bi, i: (bi, 0, 0, 0, 0))]
        args += list(bias)
    if sel is not None:
        in_specs.append(pl.BlockSpec((1, tq, LANES), lambda bi, i: (bi, i, 0)))
        args.append(sel)
    body = functools.partial(_attn_prompt_body, tq=tq, tk=tk, dq=dq, dv=dv, kcols=kcols, vcols=vcols, scale=scale,
                             mode=mode, window=window, has_bias=bias is not None, has_sel=sel is not None)
    state = (nh * tq, 1) if mode == "stick" else (nh, tq, 1)
    return pl.pallas_call(
        body,
        grid=(b, t // tq),
        in_specs=in_specs,
        out_specs=pl.BlockSpec((1, tq, nh * dv), lambda bi, i: (bi, i, 0)),
        out_shape=jax.ShapeDtypeStruct((b, t, nh * dv), F32),
        scratch_shapes=[pltpu.VMEM((nh * tq, dv), F32), pltpu.VMEM(state, F32), pltpu.VMEM(state, F32)],
        compiler_params=_cparams("parallel", "arbitrary"),
        name="attn_prompt_" + mode + ("_b" if bias is not None else "") + ("_s" if sel is not None else "")
             + ("_w" if window is not None else ""),
    )(*args)


def _compress_prompt_body(x_ref, o_ref, *, rows):
    nb = rows // NSA_CMP_BLOCK
    pool = (lax.broadcasted_iota(jnp.int32, (nb, rows), 1) // NSA_CMP_BLOCK
            == lax.broadcasted_iota(jnp.int32, (nb, rows), 0)).astype(BF16)
    o_ref[0] = _pool_dot(pool, x_ref[0]) * (1.0 / NSA_CMP_BLOCK)


def _compress_prompt(x, rows=1024):
    b, t, c = x.shape
    return pl.pallas_call(
        functools.partial(_compress_prompt_body, rows=rows),
        grid=(b, t // rows),
        in_specs=[pl.BlockSpec((1, rows, c), lambda bi, i: (bi, i, 0))],
        out_specs=pl.BlockSpec((1, rows // NSA_CMP_BLOCK, c), lambda bi, i: (bi, i, 0)),
        out_shape=jax.ShapeDtypeStruct((b, t // NSA_CMP_BLOCK, c), F32),
        compiler_params=_cparams("parallel", "parallel"),
        name="compress_prompt",
    )(x)


def _topk_mask(score, lanes, k):
    lane = lax.broadcasted_iota(jnp.int32, score.shape, score.ndim - 1)
    rank = jnp.zeros(score.shape, F32)
    for c in lanes:
        col = score[:, c:c + 1]
        tie = jnp.where(lane > c, 1.0, 0.0)
        rank = rank + jnp.where(col > score, 1.0, jnp.where(col == score, tie, 0.0))
    return jnp.where(rank < k, 1.0, 0.0)


def _cmp_core(q, cmp, qpos, n_lanes, rows_per_head, k_sel):
    nh, r = N_HEADS_GROUP, rows_per_head
    lane = lax.broadcasted_iota(jnp.int32, (r, n_lanes), 1)
    s = _bdot_nt(q, cmp[:, :HEAD_DIM]) * (HEAD_DIM ** -0.5)
    s = s.reshape(nh, r, n_lanes)
    valid = (lane * NSA_CMP_BLOCK + (NSA_CMP_BLOCK - 1) <= qpos)[None]
    s = jnp.where(valid, s, NEG)
    m = jnp.max(s, axis=-1, keepdims=True)
    e = jnp.where(valid, jnp.exp(s - m), 0.0)
    den = jnp.sum(e, axis=-1, keepdims=True)
    p = e / jnp.where(den > 0.0, den, 1.0)
    o_c = _bdot(p.reshape(nh * r, n_lanes), cmp[:, HEAD_DIM:])
    imp = p[0] + p[1] + p[2] + p[3]
    imp = imp + pltpu.roll(imp, n_lanes - 1, 1)
    ratio = NSA_SEL_BLOCK // NSA_CMP_BLOCK
    blk = lane // ratio
    cur = qpos // NSA_SEL_BLOCK
    forced = (blk == 0) | (blk == cur) | (blk == cur - 1)
    future = blk * NSA_SEL_BLOCK > qpos
    score = jnp.where(forced, NSA_FORCED_SCORE, jnp.where(future, -1.0, imp))
    even = (lane % ratio) == 0
    score = jnp.where(even, score, -2.0)
    sel = _topk_mask(score, range(0, n_lanes, ratio), k_sel) * jnp.where(even, 1.0, 0.0)
    return o_c, sel + pltpu.roll(sel, 1, 1)


def _cmp_prompt_body(q_ref, cmp_ref, oc_ref, sel_ref, *, tq):
    nh = N_HEADS_GROUP
    i = pl.program_id(1)
    n_lanes = cmp_ref.shape[1]
    q = jnp.concatenate([q_ref[0, :, h * HEAD_DIM:(h + 1) * HEAD_DIM] for h in range(nh)], axis=0).astype(BF16)
    qpos = i * tq + lax.broadcasted_iota(jnp.int32, (tq, n_lanes), 0)
    o_c, sel = _cmp_core(q, cmp_ref[0], qpos, n_lanes, tq, NSA_TOPK)
    for h in range(nh):
        oc_ref[0, :, h * HEAD_DIM:(h + 1) * HEAD_DIM] = o_c[h * tq:(h + 1) * tq]
    sel_ref[0] = sel


def _cmp_prompt(q, cmp, tq=128):
    b, t, w = q.shape
    n_lanes = cmp.shape[1]
    return pl.pallas_call(
        functools.partial(_cmp_prompt_body, tq=tq),
        grid=(b, t // tq),
        in_specs=[pl.BlockSpec((1, tq, w), lambda bi, i: (bi, i, 0)),
                  pl.BlockSpec((1, n_lanes, 2 * HEAD_DIM), lambda bi, i: (bi, 0, 0))],
        out_specs=[pl.BlockSpec((1, tq, w), lambda bi, i: (bi, i, 0)),
                   pl.BlockSpec((1, tq, n_lanes), lambda bi, i: (bi, i, 0))],
        out_shape=[jax.ShapeDtypeStruct((b, t, w), F32), jax.ShapeDtypeStruct((b, t, n_lanes), F32)],
        compiler_params=_cparams("parallel", "parallel"),
        name="cmp_prompt",
    )(q, cmp)


def _cmp_decode_body(q_ref, cmp_ref, oc_ref, sel_ref, *, past_len, n_new_forced):
    n_lanes = cmp_ref.shape[1]
    qpos = past_len + lax.broadcasted_iota(jnp.int32, (DEC_ROWS, n_lanes), 0)
    o_c, sel = _cmp_core(q_ref[0].astype(BF16), cmp_ref[0], qpos, n_lanes, DEC_ROWS, NSA_TOPK - n_new_forced)
    oc_ref[0] = o_c
    sel_ref[0] = sel


def _cmp_decode(q, cmp, past_len):
    b, r, d = q.shape
    n_lanes = cmp.shape[1]
    return pl.pallas_call(
        functools.partial(_cmp_decode_body, past_len=past_len, n_new_forced=1),
        grid=(b,),
        in_specs=[pl.BlockSpec((1, r, d), lambda bi: (bi, 0, 0)),
                  pl.BlockSpec((1, n_lanes, 2 * HEAD_DIM), lambda bi: (bi, 0, 0))],
        out_specs=[pl.BlockSpec((1, r, d), lambda bi: (bi, 0, 0)),
                   pl.BlockSpec((1, DEC_ROWS, n_lanes), lambda bi: (bi, 0, 0))],
        out_shape=[jax.ShapeDtypeStruct((b, r, d), F32), jax.ShapeDtypeStruct((b, DEC_ROWS, n_lanes), F32)],
        compiler_params=_cparams("parallel"),
        name="cmp_decode",
    )(q, cmp)


def _page_specs(block, pps, nchunks, reverse):
    def spec(k):
        def index(bi, c, pt):
            chunk = (nchunks - 1 - c) if reverse else c
            return (pt[bi, chunk * pps + k],) + (0,) * (len(block) - 1)
        return pl.BlockSpec(block, index)
    return [spec(k) for k in range(pps)]


def _compress_decode_body(pt_ref, *refs, pps):
    del pt_ref
    o_ref = refs[pps]
    pool = (lax.broadcasted_iota(jnp.int32, (DEC_ROWS, PAGE_SIZE), 1) // NSA_CMP_BLOCK
            == lax.broadcasted_iota(jnp.int32, (DEC_ROWS, PAGE_SIZE), 0)).astype(BF16)
    for k in range(pps):
        o_ref[0, k] = _pool_dot(pool, refs[k][0]) * (1.0 / NSA_CMP_BLOCK)


def _compress_decode(pool, page_table, pps=16):
    b, n_pages = page_table.shape
    c = pool.shape[-1]
    nchunks = n_pages // pps
    grid_spec = pltpu.PrefetchScalarGridSpec(
        num_scalar_prefetch=1,
        grid=(b, nchunks),
        in_specs=_page_specs((1, PAGE_SIZE, c), pps, nchunks, False),
        out_specs=pl.BlockSpec((1, pps, DEC_ROWS, c), lambda bi, ci, pt: (bi, ci, 0, 0)),
    )
    return pl.pallas_call(
        functools.partial(_compress_decode_body, pps=pps),
        grid_spec=grid_spec,
        out_shape=jax.ShapeDtypeStruct((b, n_pages, DEC_ROWS, c), F32),
        compiler_params=_cparams("parallel", "parallel"),
        name="compress_decode",
    )(page_table, *([pool] * pps))


def _gather_rows8_body(pt_ref, *refs, pps):
    del pt_ref
    o_ref = refs[pps]
    for k in range(pps):
        o_ref[0, :, k * PAGE_SIZE:(k + 1) * PAGE_SIZE] = refs[k][0]


def _gather_rows8(pool_t, page_table, pps=16):
    b, n_pages = page_table.shape
    nchunks = n_pages // pps
    grid_spec = pltpu.PrefetchScalarGridSpec(
        num_scalar_prefetch=1,
        grid=(b, nchunks),
        in_specs=_page_specs((1, DEC_ROWS, PAGE_SIZE), pps, nchunks, False),
        out_specs=pl.BlockSpec((1, DEC_ROWS, pps * PAGE_SIZE), lambda bi, ci, pt: (bi, 0, ci)),
    )
    return pl.pallas_call(
        functools.partial(_gather_rows8_body, pps=pps),
        grid_spec=grid_spec,
        out_shape=jax.ShapeDtypeStruct((b, DEC_ROWS, n_pages * PAGE_SIZE), F32),
        compiler_params=_cparams("parallel", "parallel"),
        name="gather_logf",
    )(page_table, *([pool_t] * pps))


def _cumsum_body(x_ref, o_ref):
    x = x_ref[0]
    n = x.shape[-1]
    lane = lax.broadcasted_iota(jnp.int32, x.shape, 1)
    sh = 1
    while sh < n:
        x = x + jnp.where(lane >= sh, pltpu.roll(x, sh, 1), 0.0)
        sh *= 2
    o_ref[0] = x


def _cumsum_lanes(x):
    g, r, n = x.shape
    return pl.pallas_call(
        _cumsum_body,
        grid=(g,),
        in_specs=[pl.BlockSpec((1, r, n), lambda i: (i, 0, 0))],
        out_specs=pl.BlockSpec((1, r, n), lambda i: (i, 0, 0)),
        out_shape=jax.ShapeDtypeStruct((g, r, n), F32),
        compiler_params=_cparams("parallel"),
        name="cumsum",
    )(x)


def _attn_decode_body(*refs, pps, kparts, vcols, scale, mode, has_ck):
    it = iter(refs)
    next(it)
    q_ref, rb_ref, cm_ref, cmn_ref = next(it), next(it), next(it), next(it)
    ck_ref = next(it) if has_ck else None
    ckn_ref = next(it) if has_ck else None
    new_ref = next(it)
    pages = [next(it) for _ in range(pps)]
    o_ref = next(it)
    acc_ref, m_ref, l_ref = next(it), next(it), next(it)
    nh = N_HEADS_GROUP
    c = pl.program_id(1)
    q = (q_ref[0] * scale).astype(BF16)
    rb = rb_ref[0]
    tri = _tri_later(PAGE_SIZE) if mode == "stick" else None

    def scores(page):
        out = None
        for s0, e0, q0 in kparts:
            term = _bdot_nt(q[:, q0:q0 + e0 - s0], page[0, :, s0:e0])
            out = term if out is None else out + term
        return out

    def attend(page_refs, cm, ck):
        s = [scores(p) for p in page_refs]
        s = s[0] if len(s) == 1 else jnp.concatenate(s, axis=1)
        cm = jnp.concatenate([cm] * nh, axis=0)
        valid = cm > MASK_CUT

        def pv(w):
            out = None
            for k, p in enumerate(page_refs):
                term = jnp.dot(w[:, k * PAGE_SIZE:(k + 1) * PAGE_SIZE], p[0, :, vcols[0]:vcols[1]].astype(BF16),
                               preferred_element_type=F32)
                out = term if out is None else out + term
            return out

        if mode == "stick":
            a = _stick_weights(s, valid, tri, l_ref, PAGE_SIZE)
            acc_ref[...] += pv(a.astype(BF16))
        else:
            s = s + rb
            if ck is not None:
                s = s - jnp.concatenate([jnp.broadcast_to(ck[h:h + 1], (DEC_ROWS, ck.shape[1])) for h in range(nh)],
                                        axis=0)
            _softmax_update(s, valid, pv, m_ref, l_ref, acc_ref)

    @pl.when(c == 0)
    def _():
        acc_ref[...] = jnp.zeros_like(acc_ref)
        l_ref[...] = jnp.zeros_like(l_ref)
        m_ref[...] = jnp.full_like(m_ref, NEG)
        attend([new_ref], cmn_ref[0], ckn_ref[0] if has_ck else None)

    attend(pages, cm_ref[0], ck_ref[0] if has_ck else None)

    @pl.when(c == pl.num_programs(1) - 1)
    def _():
        if mode == "stick":
            o_ref[0] = acc_ref[...]
        else:
            l = l_ref[...]
            o_ref[0] = acc_ref[...] / jnp.where(l > 0.0, l, 1.0)


def _attn_decode(q, pool, page_table, new_page, rb, cm, ck=None, *, kparts, vcols, scale, mode="softmax", pps=32):
    b, r, dq = q.shape
    n_pages = page_table.shape[1]
    c = pool.shape[-1]
    nchunks = n_pages // pps
    dv = vcols[1] - vcols[0]
    rbb = (lambda bi: bi) if rb.shape[0] > 1 else (lambda bi: 0)
    cmb = (lambda bi: bi) if cm.shape[0] > 1 else (lambda bi: 0)
    in_specs = [
        pl.BlockSpec((1, r, dq), lambda bi, ci, pt: (bi, 0, 0)),
        pl.BlockSpec((1, r, 1), lambda bi, ci, pt: (rbb(bi), 0, 0)),
        pl.BlockSpec((1, DEC_ROWS, pps * PAGE_SIZE), lambda bi, ci, pt: (cmb(bi), 0, nchunks - 1 - ci)),
        pl.BlockSpec((1, DEC_ROWS, PAGE_SIZE), lambda bi, ci, pt: (cmb(bi), 0, n_pages)),
    ]
    args = [q, rb, cm, cm]
    if ck is not None:
        in_specs += [pl.BlockSpec((1, DEC_ROWS, pps * PAGE_SIZE), lambda bi, ci, pt: (bi, 0, nchunks - 1 - ci)),
                     pl.BlockSpec((1, DEC_ROWS, PAGE_SIZE), lambda bi, ci, pt: (bi, 0, n_pages))]
        args += [ck, ck]
    in_specs += [pl.BlockSpec((1, PAGE_SIZE, c), lambda bi, ci, pt: (bi, 0, 0))]
    in_specs += _page_specs((1, PAGE_SIZE, c), pps, nchunks, True)
    args += [new_page] + [pool] * pps
    grid_spec = pltpu.PrefetchScalarGridSpec(
        num_scalar_prefetch=1,
        grid=(b, nchunks),
        in_specs=in_specs,
        out_specs=pl.BlockSpec((1, r, dv), lambda bi, ci, pt: (bi, 0, 0)),
        scratch_shapes=[pltpu.VMEM((r, dv), F32), pltpu.VMEM((r, 1), F32), pltpu.VMEM((r, 1), F32)],
    )
    body = functools.partial(_attn_decode_body, pps=pps, kparts=kparts, vcols=vcols, scale=scale, mode=mode,
                             has_ck=ck is not None)
    return pl.pallas_call(
        body,
        grid_spec=grid_spec,
        out_shape=jax.ShapeDtypeStruct((b, r, dv), F32),
        compiler_params=_cparams("parallel", "arbitrary"),
        name="attn_decode_" + mode + ("_mla" if len(kparts) > 1 else "") + ("_ck" if ck is not None else ""),
    )(page_table, *args)


_IN_SIZES = (MLA_Q_RANK, MLA_KV_RANK, MLA_ROPE, GROUP_WIDTH, 2 * HEAD_DIM, 2 * HEAD_DIM, 2 * HEAD_DIM,
             3 * N_HEADS_GROUP, GROUP_WIDTH, 2 * HEAD_DIM, GROUP_WIDTH, 2 * HEAD_DIM, N_HEADS_GROUP)
_GATE_LANES = 3 * N_HEADS_GROUP
FFN_TILE = 512


def _tile(n, pref):
    for t in range(min(pref, n), 7, -1):
        if n % t == 0 and t % 8 == 0:
            return t
    raise ValueError(f"no tile for {n}")


def _rope_tables(pos):
    posf = pos.astype(F32)[:, None]

    def cos_sin(half):
        inv = ROPE_THETA ** (-jnp.arange(half, dtype=F32) / half)
        ang = posf * inv[None, :]
        return jnp.cos(ang), jnp.sin(ang)

    c, s = cos_sin(HEAD_DIM // 2)
    c1, s1 = jnp.concatenate([c, c], axis=1), jnp.concatenate([-s, s], axis=1)
    c, s = cos_sin(MLA_ROPE // 2)
    z = jnp.zeros_like(s)
    c2 = jnp.concatenate([c, c, z, z], axis=1)
    s2a = jnp.concatenate([-s, z, z, z], axis=1)
    s2b = jnp.concatenate([z, s, z, z], axis=1)
    return c1, s1, c2, s2a, s2b


def _relayout_w_in(w):
    d = w.shape[0]
    parts, start = [], 0
    for size in _IN_SIZES:
        parts.append(w[:, start:start + size])
        start += size
    dq, dkv, kr, nq, ncmp, nsel, nwin, ngate, sq, skv, fq, fkv, fg = parts
    zeros = lambda n: jnp.zeros((d, n), w.dtype)
    out = jnp.concatenate([dq, dkv, kr, zeros(LANES - MLA_ROPE), nq, ncmp, nsel, nwin, sq, skv, fq, fkv,
                           ngate, fg, zeros(LANES - _GATE_LANES - N_HEADS_GROUP)], axis=1)
    assert out.shape[1] == N_IN_PAD
    return out.astype(BF16)


def _pad_to(a, axis, n):
    pad = [(0, 0)] * a.ndim
    pad[axis] = (0, n - a.shape[axis])
    return jnp.pad(a, pad)


def _gain_rows(rows, width=MLA_Q_RANK, n_rows=16):
    out = jnp.stack([_pad_to(r, 0, width) for r in rows], axis=0)
    return _pad_to(out, 0, n_rows)


def kernel(x_prompt, x_sample, cache_mla, cache_nsa_cmp, cache_nsa_sel, state_nsa_win, cache_sb, cache_fox_kv,
           cache_fox_logf, page_table, attn_norm, ffn_norm, ffn_w_in, ffn_w_out, w_in, w_out, mla_q_norm, mla_w_uq,
           mla_qk_norm, mla_kv_norm, mla_kr_norm, mla_w_uk, mla_w_uv, nsa_q_norm, nsa_k_norm, fox_q_norm, fox_k_norm,
           fox_f_bias):
    nb, t, d = x_prompt.shape
    db, ts, _ = x_sample.shape
    depth = w_in.shape[0]
    n_pages = page_table.shape[1]
    past = n_pages * PAGE_SIZE
    n_p, n_s = nb * t, db * ts
    nh = N_HEADS_GROUP
    assert ts <= DEC_ROWS and ts <= NSA_SEL_BLOCK and past % NSA_SEL_BLOCK == 0 and t % LANES == 0
    wb = state_nsa_win.shape[2]
    assert wb == NSA_WINDOW and wb % PAGE_SIZE == 0
    d_ff = ffn_w_out.shape[2]
    ffp = -(-d_ff // FFN_TILE) * FFN_TILE
    page_table = page_table.astype(jnp.int32)

    x = jnp.concatenate([x_prompt.reshape(n_p, d), x_sample.reshape(n_s, d)], axis=0)
    n_t = n_p + n_s
    pos = jnp.concatenate([jnp.tile(jnp.arange(t, dtype=jnp.int32), nb),
                           jnp.tile(past + jnp.arange(ts, dtype=jnp.int32), db)])
    tabs = _rope_tables(pos)
    tm_ffn, tm_proj, tq = _tile(n_t, 512), _tile(n_t, 256), LANES

    n_k = (n_pages + 1) * PAGE_SIZE
    kpos = jnp.arange(n_k, dtype=jnp.int32)[None, :]
    qpos8 = past + jnp.arange(DEC_ROWS, dtype=jnp.int32)[:, None]
    causal8 = kpos <= qpos8
    as_bias = lambda m: jnp.where(m, 0.0, NEG).astype(F32)
    cm_causal = as_bias(causal8)[None]
    cm_strict = as_bias(kpos < qpos8)[None]
    wpos = (past - wb + jnp.arange(wb + PAGE_SIZE, dtype=jnp.int32))[None, :]
    wdiff = qpos8 - wpos
    cm_window = as_bias((wdiff >= 0) & (wdiff < NSA_WINDOW))[None]
    rb_zero = jnp.zeros((1, nh * DEC_ROWS, 1), F32)
    n_pool = cache_mla.shape[1]
    flat = lambda a: a.reshape((a.shape[0] * a.shape[1],) + a.shape[2:])
    pool_mla, pool_cmp, pool_sel = flat(cache_mla), flat(cache_nsa_cmp), flat(cache_nsa_sel)
    pool_sb, pool_fox = flat(cache_sb), flat(cache_fox_kv)
    win_pages = wb // PAGE_SIZE
    pool_win = state_nsa_win.reshape(depth * db * win_pages, PAGE_SIZE, 2 * HEAD_DIM)
    win_table = jnp.arange(db * win_pages, dtype=jnp.int32).reshape(db, win_pages)
    pool_logf = flat(_pad_to(jnp.transpose(cache_fox_logf, (0, 1, 3, 2)), 2, DEC_ROWS))

    def dec_q(a, dq):
        a = a[n_p:].reshape(db, ts, nh, dq).transpose(0, 2, 1, 3)
        return _pad_to(a, 2, DEC_ROWS).reshape(db, nh * DEC_ROWS, dq)

    def dec_out(o):
        dv = o.shape[-1]
        return o.reshape(db, nh, DEC_ROWS, dv)[:, :, :ts].transpose(0, 2, 1, 3).reshape(n_s, nh * dv)

    def prompt(a):
        return a[:n_p].reshape(nb, t, a.shape[-1])

    def sample(a):
        return a[n_p:].reshape(db, ts, a.shape[-1])

    def new_page(a):
        return _pad_to(sample(a), 1, PAGE_SIZE)

    st_p = [[] for _ in range(7)]
    st_s = [[] for _ in range(7)]
    scale = HEAD_DIM ** -0.5
    kv_std = dict(kparts=((0, HEAD_DIM, 0),), vcols=(HEAD_DIM, 2 * HEAD_DIM), scale=scale)
    for l in range(depth):
        def ffn(x, i):
            wa = _pad_to(ffn_w_in[l, i][:, :d_ff], 1, ffp).astype(BF16)
            wb_ = _pad_to(ffn_w_in[l, i][:, d_ff:], 1, ffp).astype(BF16)
            wo = _pad_to(ffn_w_out[l, i], 0, ffp).astype(BF16)
            return _ffn(x, ffn_norm[l, i][None, :], wa, wb_, wo, tm_ffn, FFN_TILE)

        x = ffn(x, 0)
        wuq = _pad_to(mla_w_uq[l].reshape(MLA_Q_RANK, nh, MLA_QK), 2, MLA_PAD).reshape(MLA_Q_RANK, nh * MLA_PAD)
        vec = _gain_rows([mla_q_norm[l], mla_qk_norm[l, :MLA_NOPE], mla_qk_norm[l, MLA_NOPE:], mla_kv_norm[l],
                          mla_kr_norm[l], nsa_q_norm[l], nsa_k_norm[l, 0], nsa_k_norm[l, 1], nsa_k_norm[l, 2],
                          fox_q_norm[l], fox_k_norm[l]])
        fb = jnp.zeros((1, LANES), F32).at[0, _GATE_LANES:_GATE_LANES + nh].set(fox_f_bias[l])
        (mlaq, mlarow, nsaq, cmprow, selrow, winrow, sbq, sbrow, foxq, foxrow, gate, logf_t) = _proj(
            x, attn_norm[l][None, :], _relayout_w_in(w_in[l]), wuq.astype(BF16),
            jnp.transpose(mla_w_uk[l], (0, 2, 1)).astype(BF16), vec, fb, tabs, tm_proj)
        mlarow = mlarow[:, :MLA_QK]
        logf = logf_t[:, _GATE_LANES:_GATE_LANES + nh]

        rows_bf = lambda a: prompt(a).astype(BF16)
        olat_p = _attn_prompt(prompt(mlaq), _pad_to(prompt(mlarow), 2, MLA_PAD).astype(BF16), dq=MLA_PAD,
                              dv=MLA_KV_RANK, kcols=(0, MLA_PAD), vcols=(0, MLA_KV_RANK), scale=MLA_QK ** -0.5,
                              tq=tq, tk=_tile(t, 512))
        std = dict(dq=HEAD_DIM, dv=HEAD_DIM, kcols=(0, HEAD_DIM), vcols=(HEAD_DIM, 2 * HEAD_DIM), scale=scale, tq=tq)
        cmp_p = _compress_prompt(prompt(cmprow), rows=min(1024, t))
        oc_p, selm_p = _cmp_prompt(prompt(nsaq), cmp_p, tq=tq)
        os_p = _attn_prompt(prompt(nsaq), rows_bf(selrow), sel=selm_p, tk=_tile(t, 512), **std)
        ow_p = _attn_prompt(prompt(nsaq), rows_bf(winrow), window=NSA_WINDOW, tk=_tile(t, 256), **std)
        osb_p = _attn_prompt(prompt(sbq), rows_bf(sbrow), mode="stick", tk=_tile(t, 256), **std)
        lf_p = _pad_to(jnp.transpose(prompt(logf), (0, 2, 1)), 1, DEC_ROWS)
        cum_p = _cumsum_lanes(lf_p)[:, :nh]
        tk_fox = _tile(t, 512)
        ofox_p = _attn_prompt(prompt(foxq), rows_bf(foxrow), tk=tk_fox,
                              bias=(cum_p[..., None], cum_p.reshape(nb, nh, t // tk_fox, 1, tk_fox)), **std)

        pt_l = page_table + l * n_pool
        olat_s = _attn_decode(dec_q(mlaq, MLA_PAD), pool_mla, pt_l, new_page(mlarow), rb_zero, cm_causal,
                              kparts=((0, MLA_KV_RANK, 0), (MLA_KV_RANK, MLA_QK, MLA_KV_RANK)),
                              vcols=(0, MLA_KV_RANK), scale=MLA_QK ** -0.5)
        nsaq_s = dec_q(nsaq, HEAD_DIM)
        cmp_s = _compress_decode(pool_cmp, pt_l)[:, :, :PAGE_SIZE // NSA_CMP_BLOCK]
        cmp_s = cmp_s.reshape(db, past // NSA_CMP_BLOCK, 2 * HEAD_DIM)
        oc_s, selm_s = _cmp_decode(nsaq_s, cmp_s, past)
        sel_keys = jnp.repeat(selm_s, NSA_CMP_BLOCK, axis=-1) > 0.5
        sel_keys = jnp.concatenate([sel_keys, jnp.ones((db, DEC_ROWS, PAGE_SIZE), bool)], axis=-1)
        os_s = _attn_decode(nsaq_s, pool_sel, pt_l, new_page(selrow), rb_zero, as_bias(sel_keys & causal8[None]),
                            **kv_std)
        ow_s = _attn_decode(nsaq_s, pool_win, win_table + l * db * win_pages, new_page(winrow), rb_zero, cm_window,
                            pps=win_pages, **kv_std)
        osb_s = _attn_decode(dec_q(sbq, HEAD_DIM), pool_sb, pt_l, new_page(sbrow), rb_zero, cm_strict,
                             mode="stick", **kv_std)
        lf_new = _pad_to(_pad_to(jnp.transpose(sample(logf), (0, 2, 1)), 1, DEC_ROWS), 2, PAGE_SIZE)
        cum_s = _cumsum_lanes(jnp.concatenate([_gather_rows8(pool_logf, pt_l), lf_new], axis=-1))
        cq = _pad_to(cum_s[:, :nh, past:past + ts], 2, DEC_ROWS).reshape(db, nh * DEC_ROWS, 1)
        ofox_s = _attn_decode(dec_q(foxq, HEAD_DIM), pool_fox, pt_l, new_page(foxrow), cq, cm_causal, cum_s,
                              **kv_std)

        both = lambda p, s: jnp.concatenate([p.reshape(n_p, p.shape[-1]), dec_out(s)], axis=0)
        x = _outproj(x, both(olat_p, olat_s), both(oc_p, oc_s), both(os_p, os_s), both(ow_p, ow_s), gate,
                     both(osb_p, osb_s), both(ofox_p, ofox_s), mla_w_uv[l].astype(BF16), w_out[l].astype(BF16),
                     tm_ffn)
        x = ffn(x, 1)

        wp = min(NSA_WINDOW, t)
        win_s = jnp.concatenate([state_nsa_win[l], sample(winrow)], axis=1)[:, -wb:]
        for lst, a in zip(st_p, (prompt(mlarow), prompt(cmprow), prompt(selrow), prompt(winrow)[:, -wp:],
                                 prompt(sbrow), prompt(foxrow), prompt(logf))):
            lst.append(a)
        for lst, a in zip(st_s, (sample(mlarow), sample(cmprow), sample(selrow), win_s,
                                 sample(sbrow), sample(foxrow), sample(logf))):
            lst.append(a)

    outs_p = [jnp.stack(s, axis=0) for s in st_p]
    outs_s = [jnp.stack(s, axis=0) for s in st_s]
    res = [x[:n_p].reshape(nb, t, d), x[n_p:].reshape(db, ts, d)]
    for p, s in zip(outs_p, outs_s):
        res += [p, s]
    return tuple(res)
```

```python
import functools

import jax
import jax.numpy as jnp
import numpy as np
from jax import lax
from jax.experimental import pallas as pl
from jax.experimental.pallas import tpu as pltpu

F32 = jnp.float32
BF16 = jnp.bfloat16

HEAD_DIM = 128
N_HEADS_GROUP = 4
GROUP_WIDTH = N_HEADS_GROUP * HEAD_DIM
MLA_Q_RANK = 384
MLA_KV_RANK = 128
MLA_NOPE = 128
MLA_ROPE = 64
MLA_QK = MLA_NOPE + MLA_ROPE
MLA_PAD = 256
NSA_CMP_BLOCK = 32
NSA_SEL_BLOCK = 64
NSA_TOPK = 16
NSA_WINDOW = 512
NSA_FORCED_SCORE = 1.0e4
ROPE_THETA = 10000.0
RMS_EPS = 1e-6
PAGE_SIZE = 128
NEG = -1.0e30
MASK_CUT = -1.0e29

LANES = 128
DEC_ROWS = 8
VMEM_LIMIT = 56 * 1024 * 1024


def _cparams(*sem):
    return pltpu.CompilerParams(dimension_semantics=sem, vmem_limit_bytes=VMEM_LIMIT)


def _bdot(a, b):
    return jnp.dot(a.astype(BF16), b.astype(BF16), preferred_element_type=F32)


def _bdot_nt(a, b):
    return lax.dot_general(a.astype(BF16), b.astype(BF16), (((1,), (1,)), ((), ())),
                           preferred_element_type=F32)


def _split_dot(a, b01):
    hi = a.astype(BF16)
    r1 = a - hi.astype(F32)
    mid = r1.astype(BF16)
    lo = (r1 - mid.astype(F32)).astype(BF16)
    dot = functools.partial(jnp.dot, preferred_element_type=F32)
    return dot(hi, b01) + dot(mid, b01) + dot(lo, b01)


def _rms(x, g, n=None):
    n = x.shape[-1] if n is None else n
    ms = jnp.sum(x * x, axis=-1, keepdims=True) * (1.0 / n)
    return x * lax.rsqrt(ms + RMS_EPS) * g


def _log_sigmoid(x):
    return jnp.minimum(x, 0.0) - jnp.log1p(jnp.exp(-jnp.abs(x)))


def _ffn_body(x_ref, g_ref, wa_ref, wb_ref, wo_ref, o_ref, xn_ref, acc_ref):
    j = pl.program_id(1)

    @pl.when(j == 0)
    def _():
        xn_ref[...] = _rms(x_ref[...], g_ref[...]).astype(BF16)
        acc_ref[...] = jnp.zeros_like(acc_ref)

    xn = xn_ref[...]
    a = jnp.dot(xn, wa_ref[...], preferred_element_type=F32)
    b = jnp.dot(xn, wb_ref[...], preferred_element_type=F32)
    act = (a * jax.nn.sigmoid(a) * b).astype(BF16)
    acc_ref[...] += jnp.dot(act, wo_ref[...], preferred_element_type=F32)

    @pl.when(j == pl.num_programs(1) - 1)
    def _():
        o_ref[...] = x_ref[...] + 0.5 * acc_ref[...]


def _ffn(x, g, wa, wb, wo, tm, tf):
    t, d = x.shape
    ff = wa.shape[1]
    return pl.pallas_call(
        _ffn_body,
        grid=(t // tm, ff // tf),
        in_specs=[
            pl.BlockSpec((tm, d), lambda i, j: (i, 0)),
            pl.BlockSpec((1, d), lambda i, j: (0, 0)),
            pl.BlockSpec((d, tf), lambda i, j: (0, j)),
            pl.BlockSpec((d, tf), lambda i, j: (0, j)),
            pl.BlockSpec((tf, d), lambda i, j: (j, 0)),
        ],
        out_specs=pl.BlockSpec((tm, d), lambda i, j: (i, 0)),
        out_shape=jax.ShapeDtypeStruct((t, d), F32),
        scratch_shapes=[pltpu.VMEM((tm, d), BF16), pltpu.VMEM((tm, d), F32)],
        compiler_params=_cparams("parallel", "arbitrary"),
        name="ffn",
    )(x, g, wa, wb, wo)


_C_DQ = 0
_C_DKV = 384
_C_KR = 512
_C_NQ = 640
_C_NCMP = 1152
_C_NSEL = 1408
_C_NWIN = 1664
_C_SQ = 1920
_C_SKV = 2432
_C_FQ = 2688
_C_FKV = 3200
_C_GATE = 3456
N_IN_PAD = 3584


def _rope128(x, c, s):
    return x * c + pltpu.roll(x, 64, 1) * s


def _rope64(x, c, sa, sb):
    return x * c + pltpu.roll(x, 96, 1) * sa + pltpu.roll(x, 32, 1) * sb


def _proj_body(x_ref, gin_ref, win_ref, wuq_ref, wuk_ref, vec_ref, fb_ref,
               c1_ref, s1_ref, c2_ref, s2a_ref, s2b_ref,
               mlaq_ref, mlarow_ref, nsaq_ref, cmp_ref, sel_ref, win_ref_o, sbq_ref, sbrow_ref,
               foxq_ref, foxrow_ref, gate_ref, logf_ref):
    xn = _rms(x_ref[...], gin_ref[...]).astype(BF16)
    z = jnp.dot(xn, win_ref[...], preferred_element_type=F32)
    c1, s1 = c1_ref[...], s1_ref[...]
    c2, s2a, s2b = c2_ref[...], s2a_ref[...], s2b_ref[...]
    vec = vec_ref[...]

    def gain(row, width=HEAD_DIM):
        return vec[row:row + 1, :width]

    dqn = _rms(z[:, _C_DQ:_C_DQ + MLA_Q_RANK], gain(0, MLA_Q_RANK))
    qh = _bdot(dqn, wuq_ref[...])
    for h in range(N_HEADS_GROUP):
        a = qh[:, h * MLA_PAD:h * MLA_PAD + MLA_NOPE]
        b = qh[:, h * MLA_PAD + MLA_NOPE:(h + 1) * MLA_PAD]
        ms = (jnp.sum(a * a, axis=-1, keepdims=True) + jnp.sum(b * b, axis=-1, keepdims=True)) * (1.0 / MLA_QK)
        rs = lax.rsqrt(ms + RMS_EPS)
        mlaq_ref[:, h * MLA_PAD:h * MLA_PAD + MLA_NOPE] = _bdot(a * rs * gain(1), wuk_ref[h])
        mlaq_ref[:, h * MLA_PAD + MLA_NOPE:(h + 1) * MLA_PAD] = _rope64(b * rs * gain(2), c2, s2a, s2b)
    mlarow_ref[:, :MLA_KV_RANK] = _rms(z[:, _C_DKV:_C_DKV + MLA_KV_RANK], gain(3))
    mlarow_ref[:, MLA_KV_RANK:] = _rope64(_rms(z[:, _C_KR:_C_KR + LANES], gain(4), MLA_ROPE), c2, s2a, s2b)
    for h in range(N_HEADS_GROUP):
        sl = slice(h * HEAD_DIM, (h + 1) * HEAD_DIM)
        nsaq_ref[:, sl] = _rope128(_rms(z[:, _C_NQ + h * HEAD_DIM:_C_NQ + (h + 1) * HEAD_DIM], gain(5)), c1, s1)
        sbq_ref[:, sl] = z[:, _C_SQ + h * HEAD_DIM:_C_SQ + (h + 1) * HEAD_DIM]
        foxq_ref[:, sl] = _rms(z[:, _C_FQ + h * HEAD_DIM:_C_FQ + (h + 1) * HEAD_DIM], gain(9))
    for i, (col, ref) in enumerate(((_C_NCMP, cmp_ref), (_C_NSEL, sel_ref), (_C_NWIN, win_ref_o))):
        ref[:, :HEAD_DIM] = _rope128(_rms(z[:, col:col + HEAD_DIM], gain(6 + i)), c1, s1)
        ref[:, HEAD_DIM:] = z[:, col + HEAD_DIM:col + 2 * HEAD_DIM]
    sbrow_ref[...] = z[:, _C_SKV:_C_SKV + 2 * HEAD_DIM]
    foxrow_ref[:, :HEAD_DIM] = _rms(z[:, _C_FKV:_C_FKV + HEAD_DIM], gain(10))
    foxrow_ref[:, HEAD_DIM:] = z[:, _C_FKV + HEAD_DIM:_C_FKV + 2 * HEAD_DIM]
    gt = z[:, _C_GATE:_C_GATE + LANES]
    gate_ref[...] = jax.nn.sigmoid(gt)
    logf_ref[...] = _log_sigmoid(gt + fb_ref[...])


_PROJ_OUT_WIDTHS = (4 * MLA_PAD, MLA_PAD, GROUP_WIDTH, 256, 256, 256, GROUP_WIDTH, 256, GROUP_WIDTH, 256,
                    LANES, LANES)


def _proj(x, gin, win, wuq, wuk, vec, fb, tabs, tm):
    t, d = x.shape
    row = lambda w: pl.BlockSpec((tm, w), lambda i: (i, 0))
    full = lambda a: pl.BlockSpec(a.shape, lambda i: (0,) * a.ndim)
    return pl.pallas_call(
        _proj_body,
        grid=(t // tm,),
        in_specs=[row(d), full(gin), full(win), full(wuq), full(wuk), full(vec), full(fb)]
                 + [row(LANES)] * 5,
        out_specs=[row(w) for w in _PROJ_OUT_WIDTHS],
        out_shape=[jax.ShapeDtypeStruct((t, w), F32) for w in _PROJ_OUT_WIDTHS],
        compiler_params=_cparams("parallel"),
        name="proj",
    )(x, gin, win, wuq, wuk, vec, fb, *tabs)


def _outproj_body(h_ref, olat_ref, oc_ref, os_ref, ow_ref, gate_ref, osb_ref, ofox_ref, wuv_ref, wout_ref, o_ref):
    acc = h_ref[...]
    gate = gate_ref[...]
    olat, oc, os_, ow = olat_ref[...], oc_ref[...], os_ref[...], ow_ref[...]
    mla, nsa = [], []
    for h in range(N_HEADS_GROUP):
        sl = slice(h * HEAD_DIM, (h + 1) * HEAD_DIM)
        mla.append(_bdot(olat[:, sl], wuv_ref[h]))
        nsa.append(gate[:, 3 * h:3 * h + 1] * oc[:, sl] + gate[:, 3 * h + 1:3 * h + 2] * os_[:, sl]
                   + gate[:, 3 * h + 2:3 * h + 3] * ow[:, sl])
    groups = (jnp.concatenate(mla, axis=1), jnp.concatenate(nsa, axis=1), osb_ref[...], ofox_ref[...])
    for g, og in enumerate(groups):
        acc = acc + _bdot(og, wout_ref[g * GROUP_WIDTH:(g + 1) * GROUP_WIDTH, :])
    o_ref[...] = acc


def _outproj(h, olat, oc, os_, ow, gate, osb, ofox, wuv, wout, tm):
    t, d = h.shape
    row = lambda w: pl.BlockSpec((tm, w), lambda i: (i, 0))
    full = lambda a: pl.BlockSpec(a.shape, lambda i: (0,) * a.ndim)
    return pl.pallas_call(
        _outproj_body,
        grid=(t // tm,),
        in_specs=[row(d)] + [row(GROUP_WIDTH)] * 4 + [row(LANES)] + [row(GROUP_WIDTH)] * 2 + [full(wuv), full(wout)],
        out_specs=row(d),
        out_shape=jax.ShapeDtypeStruct((t, d), F32),
        compiler_params=_cparams("parallel"),
        name="outproj",
    )(h, olat, oc, os_, ow, gate, osb, ofox, wuv, wout)


def _split2_dot(a, b01):
    hi = a.astype(BF16)
    lo = (a - hi.astype(F32)).astype(BF16)
    return (jnp.dot(hi, b01, preferred_element_type=F32) + jnp.dot(lo, b01, preferred_element_type=F32))


def _pool_dot(p01, x):
    hi = x.astype(BF16)
    lo = (x - hi.astype(F32)).astype(BF16)
    return (jnp.dot(p01, hi, preferred_element_type=F32) + jnp.dot(p01, lo, preferred_element_type=F32))


def _softplus(z):
    return jnp.maximum(z, 0.0) + jnp.log1p(jnp.exp(-jnp.abs(z)))


def _softmax_update(s, valid, pv, m_ref, l_ref, acc_ref):
    n = s.shape[-1]
    if valid is not None:
        s = jnp.where(valid, s, NEG)
    m_prev = m_ref[...]
    m_new = jnp.maximum(m_prev, jnp.max(s, axis=-1, keepdims=True))
    alpha = jnp.exp(m_prev - m_new)
    p = jnp.exp(s - m_new)
    if valid is not None:
        p = jnp.where(valid, p, 0.0)
    l_ref[...] = alpha * l_ref[...] + jnp.sum(p, axis=-1, keepdims=True)
    rows = acc_ref.shape[0]
    acc_ref[...] = acc_ref[...] * alpha.reshape(rows, 1) + pv(p.reshape(rows, n).astype(BF16))
    m_ref[...] = m_new


def _tri_later(n):
    r = lax.broadcasted_iota(jnp.int32, (n, n), 0)
    c = lax.broadcasted_iota(jnp.int32, (n, n), 1)
    return (r > c).astype(BF16)


def _stick_weights(z, valid, tri, carry_ref, width):
    n = z.shape[-1]
    sp = _softplus(z)
    lf = -sp if valid is None else jnp.where(valid, -sp, 0.0)
    pieces = [lf[:, k * width:(k + 1) * width] for k in range(n // width)]
    run = carry_ref[...]
    between = [None] * len(pieces)
    for k in reversed(range(len(pieces))):
        between[k] = _split2_dot(pieces[k], tri) + run
        run = run + jnp.sum(pieces[k], axis=-1, keepdims=True)
    carry_ref[...] = run
    between = between[0] if len(between) == 1 else jnp.concatenate(between, axis=1)
    a = jnp.exp(z - sp + between)
    return a if valid is None else jnp.where(valid, a, 0.0)


def _attn_prompt_body(*refs, tq, tk, dq, dv, kcols, vcols, scale, mode, window, has_bias, has_sel):
    it = iter(refs)
    q_ref, kv_ref = next(it), next(it)
    rb_ref = next(it) if has_bias else None
    cb_ref = next(it) if has_bias else None
    sel_ref = next(it) if has_sel else None
    o_ref = next(it)
    acc_ref, m_ref, l_ref = next(it), next(it), next(it)
    nh = N_HEADS_GROUP
    strict = mode == "stick"
    q0 = pl.program_id(1) * tq
    q = jnp.concatenate([q_ref[0, :, h * dq:(h + 1) * dq] for h in range(nh)], axis=0)
    q = (q * scale).astype(BF16)
    acc_ref[...] = jnp.zeros_like(acc_ref)
    l_ref[...] = jnp.zeros_like(l_ref)
    m_ref[...] = jnp.full_like(m_ref, NEG)
    delta = lax.broadcasted_iota(jnp.int32, (tq, tk), 0) - lax.broadcasted_iota(jnp.int32, (tq, tk), 1)
    selm = sel_ref[0].astype(BF16) if has_sel else None
    tri = _tri_later(tk) if strict else None
    n_hi = (q0 + tq - 1) // tk + 1
    lo = 0 if window is None else jnp.maximum(q0 - (window - 1), 0) // tk
    all_masked = has_sel or window is not None
    n_full = lo if all_masked else (q0 + (0 if strict else 1)) // tk

    def block(j, masked):
        off = pl.multiple_of(j * tk, tk)
        kv = kv_ref[0, pl.ds(off, tk), :]
        v = kv[:, vcols[0]:vcols[1]].astype(BF16)
        s = _bdot_nt(q, kv[:, kcols[0]:kcols[1]])
        valid = None
        if masked:
            d = delta + (q0 - off)
            valid = (d > 0) if strict else (d >= 0)
            if window is not None:
                valid = valid & (d < window)
            if has_sel:
                lane_blk = (off + lax.broadcasted_iota(jnp.int32, (LANES, tk), 1)) // NSA_CMP_BLOCK
                expand = (lax.broadcasted_iota(jnp.int32, (LANES, tk), 0) == lane_blk).astype(BF16)
                valid = valid & (jnp.dot(selm, expand, preferred_element_type=F32) > 0.5)
        if strict:
            if masked:
                valid = jnp.broadcast_to(valid[None], (nh, tq, tk)).reshape(nh * tq, tk)
            a = _stick_weights(s, valid, tri, l_ref, tk)
            acc_ref[...] += jnp.dot(a.astype(BF16), v, preferred_element_type=F32)
        else:
            s = s.reshape(nh, tq, tk)
            if has_bias:
                s = s + rb_ref[0] - cb_ref[0, :, j]
            _softmax_update(s, valid[None] if masked else None,
                            lambda p: jnp.dot(p, v, preferred_element_type=F32), m_ref, l_ref, acc_ref)

    def masked_step(jj, carry):
        block(n_hi - 1 - jj, True)
        return carry

    def full_step(jj, carry):
        block(n_full - 1 - jj, False)
        return carry

    lax.fori_loop(0, n_hi - n_full, masked_step, 0)
    if not all_masked:
        lax.fori_loop(0, n_full - lo, full_step, 0)
    if strict:
        out = acc_ref[...]
    else:
        l = l_ref[...].reshape(nh * tq, 1)
        out = acc_ref[...] / jnp.where(l > 0.0, l, 1.0)
    for h in range(nh):
        o_ref[0, :, h * dv:(h + 1) * dv] = out[h * tq:(h + 1) * tq]


def _attn_prompt(q, kv, *, dq, dv, kcols, vcols, scale, mode="softmax", window=None, bias=None, sel=None,
                 tq=128, tk=512):
    b, t, _ = q.shape
    nh = N_HEADS_GROUP
    c = kv.shape[-1]
    in_specs = [pl.BlockSpec((1, tq, nh * dq), lambda bi, i: (bi, i, 0)),
                pl.BlockSpec((1, t, c), lambda bi, i: (bi, 0, 0))]
    args = [q, kv]
    if bias is not None:
        in_specs += [pl.BlockSpec((1, nh, tq, 1), lambda bi, i: (bi, 0, i, 0)),
                     pl.BlockSpec((1, nh, t // tk, 1, tk), lambda bi, i: (bi, 0, 0, 0, 0))]
        args += list(bias)
    if sel is not None:
        in_specs.append(pl.BlockSpec((1, tq, LANES), lambda bi, i: (bi, i, 0)))
        args.append(sel)
    body = functools.partial(_attn_prompt_body, tq=tq, tk=tk, dq=dq, dv=dv, kcols=kcols, vcols=vcols, scale=scale,
                             mode=mode, window=window, has_bias=bias is not None, has_sel=sel is not None)
    state = (nh * tq, 1) if mode == "stick" else (nh, tq, 1)
    return pl.pallas_call(
        body,
        grid=(b, t // tq),
        in_specs=in_specs,
        out_specs=pl.BlockSpec((1, tq, nh * dv), lambda bi, i: (bi, i, 0)),
        out_shape=jax.ShapeDtypeStruct((b, t, nh * dv), F32),
        scratch_shapes=[pltpu.VMEM((nh * tq, dv), F32), pltpu.VMEM(state, F32), pltpu.VMEM(state, F32)],
        compiler_params=_cparams("parallel", "arbitrary"),
        name="attn_prompt_" + mode + ("_b" if bias is not None else "") + ("_s" if sel is not None else "")
             + ("_w" if window is not None else ""),
    )(*args)


def _compress_prompt_body(x_ref, o_ref, *, rows):
    nb = rows // NSA_CMP_BLOCK
    pool = (lax.broadcasted_iota(jnp.int32, (nb, rows), 1) // NSA_CMP_BLOCK
            == lax.broadcasted_iota(jnp.int32, (nb, rows), 0)).astype(BF16)
    o_ref[0] = _pool_dot(pool, x_ref[0]) * (1.0 / NSA_CMP_BLOCK)


def _compress_prompt(x, rows=1024):
    b, t, c = x.shape
    return pl.pallas_call(
        functools.partial(_compress_prompt_body, rows=rows),
        grid=(b, t // rows),
        in_specs=[pl.BlockSpec((1, rows, c), lambda bi, i: (bi, i, 0))],
        out_specs=pl.BlockSpec((1, rows // NSA_CMP_BLOCK, c), lambda bi, i: (bi, i, 0)),
        out_shape=jax.ShapeDtypeStruct((b, t // NSA_CMP_BLOCK, c), F32),
        compiler_params=_cparams("parallel", "parallel"),
        name="compress_prompt",
    )(x)


def _topk_mask(score, lanes, k):
    lane = lax.broadcasted_iota(jnp.int32, score.shape, score.ndim - 1)
    rank = jnp.zeros(score.shape, F32)
    for c in lanes:
        col = score[:, c:c + 1]
        tie = jnp.where(lane > c, 1.0, 0.0)
        rank = rank + jnp.where(col > score, 1.0, jnp.where(col == score, tie, 0.0))
    return jnp.where(rank < k, 1.0, 0.0)


def _cmp_core(q, cmp, qpos, n_lanes, rows_per_head, k_sel):
    nh, r = N_HEADS_GROUP, rows_per_head
    lane = lax.broadcasted_iota(jnp.int32, (r, n_lanes), 1)
    s = _bdot_nt(q, cmp[:, :HEAD_DIM]) * (HEAD_DIM ** -0.5)
    s = s.reshape(nh, r, n_lanes)
    valid = (lane * NSA_CMP_BLOCK + (NSA_CMP_BLOCK - 1) <= qpos)[None]
    s = jnp.where(valid, s, NEG)
    m = jnp.max(s, axis=-1, keepdims=True)
    e = jnp.where(valid, jnp.exp(s - m), 0.0)
    den = jnp.sum(e, axis=-1, keepdims=True)
    p = e / jnp.where(den > 0.0, den, 1.0)
    o_c = _bdot(p.reshape(nh * r, n_lanes), cmp[:, HEAD_DIM:])
    imp = p[0] + p[1] + p[2] + p[3]
    imp = imp + pltpu.roll(imp, n_lanes - 1, 1)
    ratio = NSA_SEL_BLOCK // NSA_CMP_BLOCK
    blk = lane // ratio
    cur = qpos // NSA_SEL_BLOCK
    forced = (blk == 0) | (blk == cur) | (blk == cur - 1)
    future = blk * NSA_SEL_BLOCK > qpos
    score = jnp.where(forced, NSA_FORCED_SCORE, jnp.where(future, -1.0, imp))
    even = (lane % ratio) == 0
    score = jnp.where(even, score, -2.0)
    sel = _topk_mask(score, range(0, n_lanes, ratio), k_sel) * jnp.where(even, 1.0, 0.0)
    return o_c, sel + pltpu.roll(sel, 1, 1)


def _cmp_prompt_body(q_ref, cmp_ref, oc_ref, sel_ref, *, tq):
    nh = N_HEADS_GROUP
    i = pl.program_id(1)
    n_lanes = cmp_ref.shape[1]
    q = jnp.concatenate([q_ref[0, :, h * HEAD_DIM:(h + 1) * HEAD_DIM] for h in range(nh)], axis=0).astype(BF16)
    qpos = i * tq + lax.broadcasted_iota(jnp.int32, (tq, n_lanes), 0)
    o_c, sel = _cmp_core(q, cmp_ref[0], qpos, n_lanes, tq, NSA_TOPK)
    for h in range(nh):
        oc_ref[0, :, h * HEAD_DIM:(h + 1) * HEAD_DIM] = o_c[h * tq:(h + 1) * tq]
    sel_ref[0] = sel


def _cmp_prompt(q, cmp, tq=128):
    b, t, w = q.shape
    n_lanes = cmp.shape[1]
    return pl.pallas_call(
        functools.partial(_cmp_prompt_body, tq=tq),
        grid=(b, t // tq),
        in_specs=[pl.BlockSpec((1, tq, w), lambda bi, i: (bi, i, 0)),
                  pl.BlockSpec((1, n_lanes, 2 * HEAD_DIM), lambda bi, i: (bi, 0, 0))],
        out_specs=[pl.BlockSpec((1, tq, w), lambda bi, i: (bi, i, 0)),
                   pl.BlockSpec((1, tq, n_lanes), lambda bi, i: (bi, i, 0))],
        out_shape=[jax.ShapeDtypeStruct((b, t, w), F32), jax.ShapeDtypeStruct((b, t, n_lanes), F32)],
        compiler_params=_cparams("parallel", "parallel"),
        name="cmp_prompt",
    )(q, cmp)


def _cmp_decode_body(q_ref, cmp_ref, oc_ref, sel_ref, *, past_len, n_new_forced):
    n_lanes = cmp_ref.shape[1]
    qpos = past_len + lax.broadcasted_iota(jnp.int32, (DEC_ROWS, n_lanes), 0)
    o_c, sel = _cmp_core(q_ref[0].astype(BF16), cmp_ref[0], qpos, n_lanes, DEC_ROWS, NSA_TOPK - n_new_forced)
    oc_ref[0] = o_c
    sel_ref[0] = sel


def _cmp_decode(q, cmp, past_len):
    b, r, d = q.shape
    n_lanes = cmp.shape[1]
    return pl.pallas_call(
        functools.partial(_cmp_decode_body, past_len=past_len, n_new_forced=1),
        grid=(b,),
        in_specs=[pl.BlockSpec((1, r, d), lambda bi: (bi, 0, 0)),
                  pl.BlockSpec((1, n_lanes, 2 * HEAD_DIM), lambda bi: (bi, 0, 0))],
        out_specs=[pl.BlockSpec((1, r, d), lambda bi: (bi, 0, 0)),
                   pl.BlockSpec((1, DEC_ROWS, n_lanes), lambda bi: (bi, 0, 0))],
        out_shape=[jax.ShapeDtypeStruct((b, r, d), F32), jax.ShapeDtypeStruct((b, DEC_ROWS, n_lanes), F32)],
        compiler_params=_cparams("parallel"),
        name="cmp_decode",
    )(q, cmp)


def _page_specs(block, pps, nchunks, reverse):
    def spec(k):
        def index(bi, c, pt):
            chunk = (nchunks - 1 - c) if reverse else c
            return (pt[bi, chunk * pps + k],) + (0,) * (len(block) - 1)
        return pl.BlockSpec(block, index)
    return [spec(k) for k in range(pps)]


def _paged_pipeline(page_src, buf, sem, n_chunks, ch, consume, first=None):
    def copy(chunk, slot, k):
        return pltpu.make_async_copy(page_src(chunk * ch + k), buf.at[slot, k], sem.at[slot, k])

    def start(chunk, slot):
        for k in range(ch):
            copy(chunk, slot, k).start()

    def wait(chunk, slot):
        for k in range(ch):
            copy(chunk, slot, k).wait()

    last = n_chunks - 1
    start(last, 0)
    if first is not None:
        first()

    def pair(i, carry):
        c0 = last - 2 * i
        start(c0 - 1, 1)
        wait(c0, 0)
        consume(0, c0)

        @pl.when(c0 - 2 >= 0)
        def _():
            start(c0 - 2, 0)

        wait(c0 - 1, 1)
        consume(1, c0 - 1)
        return carry

    lax.fori_loop(0, n_chunks // 2, pair, 0)


def _compress_decode_body(pt_ref, pool_ref, o_ref, buf, sem, *, layer, n_chunks, ch):
    b = pl.program_id(0)
    pool = (lax.broadcasted_iota(jnp.int32, (DEC_ROWS, PAGE_SIZE), 1) // NSA_CMP_BLOCK
            == lax.broadcasted_iota(jnp.int32, (DEC_ROWS, PAGE_SIZE), 0)).astype(BF16)

    def consume(slot, chunk):
        for k in range(ch):
            o_ref[0, chunk * ch + k] = _pool_dot(pool, buf[slot, k]) * (1.0 / NSA_CMP_BLOCK)

    _paged_pipeline(lambda j: pool_ref.at[layer, pt_ref[b, j]], buf, sem, n_chunks, ch, consume)


def _compress_decode(pool, layer, page_table, ch=16):
    b, n_pages = page_table.shape
    c = pool.shape[-1]
    n_chunks = n_pages // ch
    assert n_chunks % 2 == 0
    grid_spec = pltpu.PrefetchScalarGridSpec(
        num_scalar_prefetch=1,
        grid=(b,),
        in_specs=[pl.BlockSpec(memory_space=pl.ANY)],
        out_specs=pl.BlockSpec((1, n_pages, DEC_ROWS, c), lambda bi, pt: (bi, 0, 0, 0)),
        scratch_shapes=[pltpu.VMEM((2, ch, PAGE_SIZE, c), F32), pltpu.SemaphoreType.DMA((2, ch))],
    )
    return pl.pallas_call(
        functools.partial(_compress_decode_body, layer=layer, n_chunks=n_chunks, ch=ch),
        grid_spec=grid_spec,
        out_shape=jax.ShapeDtypeStruct((b, n_pages, DEC_ROWS, c), F32),
        compiler_params=_cparams("parallel"),
        name="compress_decode",
    )(page_table, pool)


def _gather_rows8_body(pt_ref, *refs, pps):
    del pt_ref
    o_ref = refs[pps]
    for k in range(pps):
        o_ref[0, :, k * PAGE_SIZE:(k + 1) * PAGE_SIZE] = refs[k][0]


def _gather_rows8(pool_t, page_table, pps=16):
    b, n_pages = page_table.shape
    nchunks = n_pages // pps
    grid_spec = pltpu.PrefetchScalarGridSpec(
        num_scalar_prefetch=1,
        grid=(b, nchunks),
        in_specs=_page_specs((1, DEC_ROWS, PAGE_SIZE), pps, nchunks, False),
        out_specs=pl.BlockSpec((1, DEC_ROWS, pps * PAGE_SIZE), lambda bi, ci, pt: (bi, 0, ci)),
    )
    return pl.pallas_call(
        functools.partial(_gather_rows8_body, pps=pps),
        grid_spec=grid_spec,
        out_shape=jax.ShapeDtypeStruct((b, DEC_ROWS, n_pages * PAGE_SIZE), F32),
        compiler_params=_cparams("parallel", "parallel"),
        name="gather_logf",
    )(page_table, *([pool_t] * pps))


def _cumsum_body(x_ref, o_ref):
    x = x_ref[0]
    n = x.shape[-1]
    lane = lax.broadcasted_iota(jnp.int32, x.shape, 1)
    sh = 1
    while sh < n:
        x = x + jnp.where(lane >= sh, pltpu.roll(x, sh, 1), 0.0)
        sh *= 2
    o_ref[0] = x


def _cumsum_lanes(x):
    g, r, n = x.shape
    return pl.pallas_call(
        _cumsum_body,
        grid=(g,),
        in_specs=[pl.BlockSpec((1, r, n), lambda i: (i, 0, 0))],
        out_specs=pl.BlockSpec((1, r, n), lambda i: (i, 0, 0)),
        out_shape=jax.ShapeDtypeStruct((g, r, n), F32),
        compiler_params=_cparams("parallel"),
        name="cumsum",
    )(x)


def _attn_decode_body(*refs, ch, n_chunks, page_src, kparts, vcols, scale, mode, has_ck):
    it = iter(refs)
    pt_ref = next(it)
    q_ref, rb_ref, cm_ref, cmn_ref = next(it), next(it), next(it), next(it)
    ck_ref = next(it) if has_ck else None
    ckn_ref = next(it) if has_ck else None
    new_ref, pool_ref = next(it), next(it)
    o_ref = next(it)
    buf, sem, acc_ref, m_ref, l_ref = next(it), next(it), next(it), next(it), next(it)
    nh = N_HEADS_GROUP
    b = pl.program_id(0)
    q = (q_ref[0] * scale).astype(BF16)
    rb = rb_ref[0]
    tri = _tri_later(PAGE_SIZE) if mode == "stick" else None

    def scores(page):
        out = None
        for s0, e0, q0 in kparts:
            term = _bdot_nt(q[:, q0:q0 + e0 - s0], page[:, s0:e0])
            out = term if out is None else out + term
        return out

    def attend(page_refs, cm, ck):
        s = [scores(p) for p in page_refs]
        s = s[0] if len(s) == 1 else jnp.concatenate(s, axis=1)
        cm = jnp.concatenate([cm] * nh, axis=0)
        valid = cm > MASK_CUT

        def pv(w):
            out = None
            for k, p in enumerate(page_refs):
                term = jnp.dot(w[:, k * PAGE_SIZE:(k + 1) * PAGE_SIZE], p[:, vcols[0]:vcols[1]].astype(BF16),
                               preferred_element_type=F32)
                out = term if out is None else out + term
            return out

        if mode == "stick":
            a = _stick_weights(s, valid, tri, l_ref, PAGE_SIZE)
            acc_ref[...] += pv(a.astype(BF16))
        else:
            s = s + rb
            if ck is not None:
                s = s - jnp.concatenate([jnp.broadcast_to(ck[h:h + 1], (DEC_ROWS, ck.shape[1])) for h in range(nh)],
                                        axis=0)
            _softmax_update(s, valid, pv, m_ref, l_ref, acc_ref)

    acc_ref[...] = jnp.zeros_like(acc_ref)
    l_ref[...] = jnp.zeros_like(l_ref)
    m_ref[...] = jnp.full_like(m_ref, NEG)

    def consume(slot, chunk):
        attend([buf.at[slot, k] for k in range(ch)], cm_ref[0, chunk], ck_ref[0, chunk] if has_ck else None)

    def new_rows():
        attend([new_ref.at[0]], cmn_ref[0], ckn_ref[0] if has_ck else None)

    _paged_pipeline(lambda j: page_src(pool_ref, pt_ref, b, j), buf, sem, n_chunks, ch, consume, first=new_rows)
    if mode == "stick":
        o_ref[0] = acc_ref[...]
    else:
        l = l_ref[...]
        o_ref[0] = acc_ref[...] / jnp.where(l > 0.0, l, 1.0)


def _attn_decode(q, pool, page_src, page_table, new_page, rb, cm, ck=None, *, kparts, vcols, scale, mode="softmax",
                 ch=32):
    b, r, dq = q.shape
    n_pages = page_table.shape[1]
    c = pool.shape[-1]
    n_chunks = n_pages // ch
    assert n_chunks % 2 == 0
    dv = vcols[1] - vcols[0]
    past = n_pages * PAGE_SIZE
    rbb = (lambda bi: bi) if rb.shape[0] > 1 else (lambda bi: 0)
    cmb = (lambda bi: bi) if cm.shape[0] > 1 else (lambda bi: 0)

    def chunked(a):
        g = a.shape[0]
        return a[:, :, :past].reshape(g, DEC_ROWS, n_chunks, ch * PAGE_SIZE).transpose(0, 2, 1, 3)

    in_specs = [
        pl.BlockSpec((1, r, dq), lambda bi, pt: (bi, 0, 0)),
        pl.BlockSpec((1, r, 1), lambda bi, pt: (rbb(bi), 0, 0)),
        pl.BlockSpec((1, n_chunks, DEC_ROWS, ch * PAGE_SIZE), lambda bi, pt: (cmb(bi), 0, 0, 0)),
        pl.BlockSpec((1, DEC_ROWS, PAGE_SIZE), lambda bi, pt: (cmb(bi), 0, 0)),
    ]
    args = [q, rb, chunked(cm), cm[:, :, past:]]
    if ck is not None:
        in_specs += [pl.BlockSpec((1, n_chunks, DEC_ROWS, ch * PAGE_SIZE), lambda bi, pt: (bi, 0, 0, 0)),
                     pl.BlockSpec((1, DEC_ROWS, PAGE_SIZE), lambda bi, pt: (bi, 0, 0))]
        args += [chunked(ck), ck[:, :, past:]]
    in_specs += [pl.BlockSpec((1, PAGE_SIZE, c), lambda bi, pt: (bi, 0, 0)), pl.BlockSpec(memory_space=pl.ANY)]
    args += [new_page, pool]
    grid_spec = pltpu.PrefetchScalarGridSpec(
        num_scalar_prefetch=1,
        grid=(b,),
        in_specs=in_specs,
        out_specs=pl.BlockSpec((1, r, dv), lambda bi, pt: (bi, 0, 0)),
        scratch_shapes=[pltpu.VMEM((2, ch, PAGE_SIZE, c), F32), pltpu.SemaphoreType.DMA((2, ch)),
                        pltpu.VMEM((r, dv), F32), pltpu.VMEM((r, 1), F32), pltpu.VMEM((r, 1), F32)],
    )
    body = functools.partial(_attn_decode_body, ch=ch, n_chunks=n_chunks, page_src=page_src, kparts=kparts,
                             vcols=vcols, scale=scale, mode=mode, has_ck=ck is not None)
    return pl.pallas_call(
        body,
        grid_spec=grid_spec,
        out_shape=jax.ShapeDtypeStruct((b, r, dv), F32),
        compiler_params=_cparams("parallel"),
        name="attn_decode_" + mode + ("_mla" if len(kparts) > 1 else "") + ("_ck" if ck is not None else ""),
    )(page_table, *args)


_IN_SIZES = (MLA_Q_RANK, MLA_KV_RANK, MLA_ROPE, GROUP_WIDTH, 2 * HEAD_DIM, 2 * HEAD_DIM, 2 * HEAD_DIM,
             3 * N_HEADS_GROUP, GROUP_WIDTH, 2 * HEAD_DIM, GROUP_WIDTH, 2 * HEAD_DIM, N_HEADS_GROUP)
_GATE_LANES = 3 * N_HEADS_GROUP
FFN_TILE = 512


def _tile(n, pref):
    for t in range(min(pref, n), 7, -1):
        if n % t == 0 and t % 8 == 0:
            return t
    raise ValueError(f"no tile for {n}")


def _rope_tables(pos):
    posf = pos.astype(F32)[:, None]

    def cos_sin(half):
        inv = ROPE_THETA ** (-jnp.arange(half, dtype=F32) / half)
        ang = posf * inv[None, :]
        return jnp.cos(ang), jnp.sin(ang)

    c, s = cos_sin(HEAD_DIM // 2)
    c1, s1 = jnp.concatenate([c, c], axis=1), jnp.concatenate([-s, s], axis=1)
    c, s = cos_sin(MLA_ROPE // 2)
    z = jnp.zeros_like(s)
    c2 = jnp.concatenate([c, c, z, z], axis=1)
    s2a = jnp.concatenate([-s, z, z, z], axis=1)
    s2b = jnp.concatenate([z, s, z, z], axis=1)
    return c1, s1, c2, s2a, s2b


def _relayout_w_in(w):
    d = w.shape[0]
    parts, start = [], 0
    for size in _IN_SIZES:
        parts.append(w[:, start:start + size])
        start += size
    dq, dkv, kr, nq, ncmp, nsel, nwin, ngate, sq, skv, fq, fkv, fg = parts
    zeros = lambda n: jnp.zeros((d, n), w.dtype)
    out = jnp.concatenate([dq, dkv, kr, zeros(LANES - MLA_ROPE), nq, ncmp, nsel, nwin, sq, skv, fq, fkv,
                           ngate, fg, zeros(LANES - _GATE_LANES - N_HEADS_GROUP)], axis=1)
    assert out.shape[1] == N_IN_PAD
    return out.astype(BF16)


def _pad_to(a, axis, n):
    pad = [(0, 0)] * a.ndim
    pad[axis] = (0, n - a.shape[axis])
    return jnp.pad(a, pad)


def _gain_rows(rows, width=MLA_Q_RANK, n_rows=16):
    out = jnp.stack([_pad_to(r, 0, width) for r in rows], axis=0)
    return _pad_to(out, 0, n_rows)


def kernel(x_prompt, x_sample, cache_mla, cache_nsa_cmp, cache_nsa_sel, state_nsa_win, cache_sb, cache_fox_kv,
           cache_fox_logf, page_table, attn_norm, ffn_norm, ffn_w_in, ffn_w_out, w_in, w_out, mla_q_norm, mla_w_uq,
           mla_qk_norm, mla_kv_norm, mla_kr_norm, mla_w_uk, mla_w_uv, nsa_q_norm, nsa_k_norm, fox_q_norm, fox_k_norm,
           fox_f_bias):
    nb, t, d = x_prompt.shape
    db, ts, _ = x_sample.shape
    depth = w_in.shape[0]
    n_pages = page_table.shape[1]
    past = n_pages * PAGE_SIZE
    n_p, n_s = nb * t, db * ts
    nh = N_HEADS_GROUP
    assert ts <= DEC_ROWS and ts <= NSA_SEL_BLOCK and past % NSA_SEL_BLOCK == 0 and t % LANES == 0
    wb = state_nsa_win.shape[2]
    assert wb == NSA_WINDOW and wb % PAGE_SIZE == 0
    d_ff = ffn_w_out.shape[2]
    ffp = -(-d_ff // FFN_TILE) * FFN_TILE
    page_table = page_table.astype(jnp.int32)

    x = jnp.concatenate([x_prompt.reshape(n_p, d), x_sample.reshape(n_s, d)], axis=0)
    n_t = n_p + n_s
    pos = jnp.concatenate([jnp.tile(jnp.arange(t, dtype=jnp.int32), nb),
                           jnp.tile(past + jnp.arange(ts, dtype=jnp.int32), db)])
    tabs = _rope_tables(pos)
    tm_ffn, tm_proj, tq = _tile(n_t, 512), _tile(n_t, 256), LANES

    n_k = (n_pages + 1) * PAGE_SIZE
    kpos = jnp.arange(n_k, dtype=jnp.int32)[None, :]
    qpos8 = past + jnp.arange(DEC_ROWS, dtype=jnp.int32)[:, None]
    causal8 = kpos <= qpos8
    as_bias = lambda m: jnp.where(m, 0.0, NEG).astype(F32)
    cm_causal = as_bias(causal8)[None]
    cm_strict = as_bias(kpos < qpos8)[None]
    wpos = (past - wb + jnp.arange(wb + PAGE_SIZE, dtype=jnp.int32))[None, :]
    wdiff = qpos8 - wpos
    cm_window = as_bias((wdiff >= 0) & (wdiff < NSA_WINDOW))[None]
    rb_zero = jnp.zeros((1, nh * DEC_ROWS, 1), F32)
    n_pool = cache_mla.shape[1]
    win_pages = wb // PAGE_SIZE
    win_table = jnp.zeros((db, win_pages), jnp.int32)
    lf_t = _pad_to(jnp.transpose(cache_fox_logf, (0, 1, 3, 2)), 2, DEC_ROWS)
    pool_logf = lf_t.reshape(depth * n_pool, DEC_ROWS, PAGE_SIZE)

    def dec_q(a, dq):
        a = a[n_p:].reshape(db, ts, nh, dq).transpose(0, 2, 1, 3)
        return _pad_to(a, 2, DEC_ROWS).reshape(db, nh * DEC_ROWS, dq)

    def dec_out(o):
        dv = o.shape[-1]
        return o.reshape(db, nh, DEC_ROWS, dv)[:, :, :ts].transpose(0, 2, 1, 3).reshape(n_s, nh * dv)

    def prompt(a):
        return a[:n_p].reshape(nb, t, a.shape[-1])

    def sample(a):
        return a[n_p:].reshape(db, ts, a.shape[-1])

    def new_page(a):
        return _pad_to(sample(a), 1, PAGE_SIZE)

    st_p = [[] for _ in range(7)]
    st_s = [[] for _ in range(7)]
    scale = HEAD_DIM ** -0.5
    kv_std = dict(kparts=((0, HEAD_DIM, 0),), vcols=(HEAD_DIM, 2 * HEAD_DIM), scale=scale)
    for l in range(depth):
        def ffn(x, i):
            wa = _pad_to(ffn_w_in[l, i][:, :d_ff], 1, ffp).astype(BF16)
            wb_ = _pad_to(ffn_w_in[l, i][:, d_ff:], 1, ffp).astype(BF16)
            wo = _pad_to(ffn_w_out[l, i], 0, ffp).astype(BF16)
            return _ffn(x, ffn_norm[l, i][None, :], wa, wb_, wo, tm_ffn, FFN_TILE)

        x = ffn(x, 0)
        wuq = _pad_to(mla_w_uq[l].reshape(MLA_Q_RANK, nh, MLA_QK), 2, MLA_PAD).reshape(MLA_Q_RANK, nh * MLA_PAD)
        vec = _gain_rows([mla_q_norm[l], mla_qk_norm[l, :MLA_NOPE], mla_qk_norm[l, MLA_NOPE:], mla_kv_norm[l],
                          mla_kr_norm[l], nsa_q_norm[l], nsa_k_norm[l, 0], nsa_k_norm[l, 1], nsa_k_norm[l, 2],
                          fox_q_norm[l], fox_k_norm[l]])
        fb = jnp.zeros((1, LANES), F32).at[0, _GATE_LANES:_GATE_LANES + nh].set(fox_f_bias[l])
        (mlaq, mlarow, nsaq, cmprow, selrow, winrow, sbq, sbrow, foxq, foxrow, gate, logf_t) = _proj(
            x, attn_norm[l][None, :], _relayout_w_in(w_in[l]), wuq.astype(BF16),
            jnp.transpose(mla_w_uk[l], (0, 2, 1)).astype(BF16), vec, fb, tabs, tm_proj)
        mlarow = mlarow[:, :MLA_QK]
        logf = logf_t[:, _GATE_LANES:_GATE_LANES + nh]

        rows_bf = lambda a: prompt(a).astype(BF16)
        olat_p = _attn_prompt(prompt(mlaq), _pad_to(prompt(mlarow), 2, MLA_PAD).astype(BF16), dq=MLA_PAD,
                              dv=MLA_KV_RANK, kcols=(0, MLA_PAD), vcols=(0, MLA_KV_RANK), scale=MLA_QK ** -0.5,
                              tq=tq, tk=_tile(t, 512))
        std = dict(dq=HEAD_DIM, dv=HEAD_DIM, kcols=(0, HEAD_DIM), vcols=(HEAD_DIM, 2 * HEAD_DIM), scale=scale, tq=tq)
        cmp_p = _compress_prompt(prompt(cmprow), rows=min(1024, t))
        oc_p, selm_p = _cmp_prompt(prompt(nsaq), cmp_p, tq=tq)
        os_p = _attn_prompt(prompt(nsaq), rows_bf(selrow), sel=selm_p, tk=_tile(t, 512), **std)
        ow_p = _attn_prompt(prompt(nsaq), rows_bf(winrow), window=NSA_WINDOW, tk=_tile(t, 256), **std)
        osb_p = _attn_prompt(prompt(sbq), rows_bf(sbrow), mode="stick", tk=_tile(t, 256), **std)
        lf_p = _pad_to(jnp.transpose(prompt(logf), (0, 2, 1)), 1, DEC_ROWS)
        cum_p = _cumsum_lanes(lf_p)[:, :nh]
        tk_fox = _tile(t, 512)
        ofox_p = _attn_prompt(prompt(foxq), rows_bf(foxrow), tk=tk_fox,
                              bias=(cum_p[..., None], cum_p.reshape(nb, nh, t // tk_fox, 1, tk_fox)), **std)

        def paged(pool_ref, pt_ref, b, j, l=l):
            return pool_ref.at[l, pt_ref[b, j]]

        def windowed(pool_ref, pt_ref, b, j, l=l):
            return pool_ref.at[l, b, pl.ds(pl.multiple_of(j * PAGE_SIZE, PAGE_SIZE), PAGE_SIZE)]

        olat_s = _attn_decode(dec_q(mlaq, MLA_PAD), cache_mla, paged, page_table, new_page(mlarow), rb_zero,
                              cm_causal, kparts=((0, MLA_KV_RANK, 0), (MLA_KV_RANK, MLA_QK, MLA_KV_RANK)),
                              vcols=(0, MLA_KV_RANK), scale=MLA_QK ** -0.5)
        nsaq_s = dec_q(nsaq, HEAD_DIM)
        cmp_s = _compress_decode(cache_nsa_cmp, l, page_table)[:, :, :PAGE_SIZE // NSA_CMP_BLOCK]
        cmp_s = cmp_s.reshape(db, past // NSA_CMP_BLOCK, 2 * HEAD_DIM)
        oc_s, selm_s = _cmp_decode(nsaq_s, cmp_s, past)
        sel_keys = jnp.repeat(selm_s, NSA_CMP_BLOCK, axis=-1) > 0.5
        sel_keys = jnp.concatenate([sel_keys, jnp.ones((db, DEC_ROWS, PAGE_SIZE), bool)], axis=-1)
        os_s = _attn_decode(nsaq_s, cache_nsa_sel, paged, page_table, new_page(selrow), rb_zero,
                            as_bias(sel_keys & causal8[None]), **kv_std)
        ow_s = _attn_decode(nsaq_s, state_nsa_win, windowed, win_table, new_page(winrow), rb_zero, cm_window,
                            ch=win_pages // 2, **kv_std)
        osb_s = _attn_decode(dec_q(sbq, HEAD_DIM), cache_sb, paged, page_table, new_page(sbrow), rb_zero, cm_strict,
                             mode="stick", **kv_std)
        lf_new = _pad_to(_pad_to(jnp.transpose(sample(logf), (0, 2, 1)), 1, DEC_ROWS), 2, PAGE_SIZE)
        lf_past = _gather_rows8(pool_logf, page_table + l * n_pool)
        cum_s = _cumsum_lanes(jnp.concatenate([lf_past, lf_new], axis=-1))
        cq = _pad_to(cum_s[:, :nh, past:past + ts], 2, DEC_ROWS).reshape(db, nh * DEC_ROWS, 1)
        ofox_s = _attn_decode(dec_q(foxq, HEAD_DIM), cache_fox_kv, paged, page_table, new_page(foxrow), cq,
                              cm_causal, cum_s, **kv_std)

        both = lambda p, s: jnp.concatenate([p.reshape(n_p, p.shape[-1]), dec_out(s)], axis=0)
        x = _outproj(x, both(olat_p, olat_s), both(oc_p, oc_s), both(os_p, os_s), both(ow_p, ow_s), gate,
                     both(osb_p, osb_s), both(ofox_p, ofox_s), mla_w_uv[l].astype(BF16), w_out[l].astype(BF16),
                     tm_ffn)
        x = ffn(x, 1)

        wp = min(NSA_WINDOW, t)
        win_s = jnp.concatenate([state_nsa_win[l], sample(winrow)], axis=1)[:, -wb:]
        for lst, a in zip(st_p, (prompt(mlarow), prompt(cmprow), prompt(selrow), prompt(winrow)[:, -wp:],
                                 prompt(sbrow), prompt(foxrow), prompt(logf))):
            lst.append(a)
        for lst, a in zip(st_s, (sample(mlarow), sample(cmprow), sample(selrow), win_s,
                                 sample(sbrow), sample(foxrow), sample(logf))):
            lst.append(a)

    outs_p = [jnp.stack(s, axis=0) for s in st_p]
    outs_s = [jnp.stack(s, axis=0) for s in st_s]
    res = [x[:n_p].reshape(nb, t, d), x[n_p:].reshape(db, ts, d)]
    for p, s in zip(outs_p, outs_s):
        res += [p, s]
    return tuple(res)
```

```python
import functools

import jax
import jax.numpy as jnp
import numpy as np
from jax import lax
from jax.experimental import pallas as pl
from jax.experimental.pallas import tpu as pltpu

F32 = jnp.float32
BF16 = jnp.bfloat16

HEAD_DIM = 128
N_HEADS_GROUP = 4
GROUP_WIDTH = N_HEADS_GROUP * HEAD_DIM
MLA_Q_RANK = 384
MLA_KV_RANK = 128
MLA_NOPE = 128
MLA_ROPE = 64
MLA_QK = MLA_NOPE + MLA_ROPE
MLA_PAD = 256
NSA_CMP_BLOCK = 32
NSA_SEL_BLOCK = 64
NSA_TOPK = 16
NSA_WINDOW = 512
NSA_FORCED_SCORE = 1.0e4
ROPE_THETA = 10000.0
RMS_EPS = 1e-6
PAGE_SIZE = 128
NEG = -1.0e30
MASK_CUT = -1.0e29

LANES = 128
DEC_ROWS = 8
VMEM_LIMIT = 56 * 1024 * 1024


def _cparams(*sem):
    return pltpu.CompilerParams(dimension_semantics=sem, vmem_limit_bytes=VMEM_LIMIT)


def _bdot(a, b):
    return jnp.dot(a.astype(BF16), b.astype(BF16), preferred_element_type=F32)


def _bdot_nt(a, b):
    return lax.dot_general(a.astype(BF16), b.astype(BF16), (((1,), (1,)), ((), ())),
                           preferred_element_type=F32)


def _split_dot(a, b01):
    hi = a.astype(BF16)
    r1 = a - hi.astype(F32)
    mid = r1.astype(BF16)
    lo = (r1 - mid.astype(F32)).astype(BF16)
    dot = functools.partial(jnp.dot, preferred_element_type=F32)
    return dot(hi, b01) + dot(mid, b01) + dot(lo, b01)


def _rms(x, g, n=None):
    n = x.shape[-1] if n is None else n
    ms = jnp.sum(x * x, axis=-1, keepdims=True) * (1.0 / n)
    return x * lax.rsqrt(ms + RMS_EPS) * g


def _log_sigmoid(x):
    return jnp.minimum(x, 0.0) - jnp.log1p(jnp.exp(-jnp.abs(x)))


def _ffn_body(x_ref, g_ref, wa_ref, wb_ref, wo_ref, o_ref, xn_ref, acc_ref):
    j = pl.program_id(1)

    @pl.when(j == 0)
    def _():
        xn_ref[...] = _rms(x_ref[...], g_ref[...]).astype(BF16)
        acc_ref[...] = jnp.zeros_like(acc_ref)

    xn = xn_ref[...]
    a = jnp.dot(xn, wa_ref[...], preferred_element_type=F32)
    b = jnp.dot(xn, wb_ref[...], preferred_element_type=F32)
    act = (a * jax.nn.sigmoid(a) * b).astype(BF16)
    acc_ref[...] += jnp.dot(act, wo_ref[...], preferred_element_type=F32)

    @pl.when(j == pl.num_programs(1) - 1)
    def _():
        o_ref[...] = x_ref[...] + 0.5 * acc_ref[...]


def _ffn(x, g, wa, wb, wo, tm, tf):
    t, d = x.shape
    ff = wa.shape[1]
    return pl.pallas_call(
        _ffn_body,
        grid=(t // tm, ff // tf),
        in_specs=[
            pl.BlockSpec((tm, d), lambda i, j: (i, 0)),
            pl.BlockSpec((1, d), lambda i, j: (0, 0)),
            pl.BlockSpec((d, tf), lambda i, j: (0, j)),
            pl.BlockSpec((d, tf), lambda i, j: (0, j)),
            pl.BlockSpec((tf, d), lambda i, j: (j, 0)),
        ],
        out_specs=pl.BlockSpec((tm, d), lambda i, j: (i, 0)),
        out_shape=jax.ShapeDtypeStruct((t, d), F32),
        scratch_shapes=[pltpu.VMEM((tm, d), BF16), pltpu.VMEM((tm, d), F32)],
        compiler_params=_cparams("parallel", "arbitrary"),
        name="ffn",
    )(x, g, wa, wb, wo)


_C_DQ = 0
_C_DKV = 384
_C_KR = 512
_C_NQ = 640
_C_NCMP = 1152
_C_NSEL = 1408
_C_NWIN = 1664
_C_SQ = 1920
_C_SKV = 2432
_C_FQ = 2688
_C_FKV = 3200
_C_GATE = 3456
N_IN_PAD = 3584


def _rope128(x, c, s):
    return x * c + pltpu.roll(x, 64, 1) * s


def _rope64(x, c, sa, sb):
    return x * c + pltpu.roll(x, 96, 1) * sa + pltpu.roll(x, 32, 1) * sb


def _proj_body(x_ref, gin_ref, win_ref, wuq_ref, wuk_ref, vec_ref, fb_ref,
               c1_ref, s1_ref, c2_ref, s2a_ref, s2b_ref,
               mlaq_ref, mlarow_ref, nsaq_ref, cmp_ref, sel_ref, win_ref_o, sbq_ref, sbrow_ref,
               foxq_ref, foxrow_ref, gate_ref, logf_ref):
    xn = _rms(x_ref[...], gin_ref[...]).astype(BF16)
    z = jnp.dot(xn, win_ref[...], preferred_element_type=F32)
    c1, s1 = c1_ref[...], s1_ref[...]
    c2, s2a, s2b = c2_ref[...], s2a_ref[...], s2b_ref[...]
    vec = vec_ref[...]

    def gain(row, width=HEAD_DIM):
        return vec[row:row + 1, :width]

    dqn = _rms(z[:, _C_DQ:_C_DQ + MLA_Q_RANK], gain(0, MLA_Q_RANK))
    qh = _bdot(dqn, wuq_ref[...])
    for h in range(N_HEADS_GROUP):
        a = qh[:, h * MLA_PAD:h * MLA_PAD + MLA_NOPE]
        b = qh[:, h * MLA_PAD + MLA_NOPE:(h + 1) * MLA_PAD]
        ms = (jnp.sum(a * a, axis=-1, keepdims=True) + jnp.sum(b * b, axis=-1, keepdims=True)) * (1.0 / MLA_QK)
        rs = lax.rsqrt(ms + RMS_EPS)
        mlaq_ref[:, h * MLA_PAD:h * MLA_PAD + MLA_NOPE] = _bdot(a * rs * gain(1), wuk_ref[h])
        mlaq_ref[:, h * MLA_PAD + MLA_NOPE:(h + 1) * MLA_PAD] = _rope64(b * rs * gain(2), c2, s2a, s2b)
    mlarow_ref[:, :MLA_KV_RANK] = _rms(z[:, _C_DKV:_C_DKV + MLA_KV_RANK], gain(3))
    mlarow_ref[:, MLA_KV_RANK:] = _rope64(_rms(z[:, _C_KR:_C_KR + LANES], gain(4), MLA_ROPE), c2, s2a, s2b)
    for h in range(N_HEADS_GROUP):
        sl = slice(h * HEAD_DIM, (h + 1) * HEAD_DIM)
        nsaq_ref[:, sl] = _rope128(_rms(z[:, _C_NQ + h * HEAD_DIM:_C_NQ + (h + 1) * HEAD_DIM], gain(5)), c1, s1)
        sbq_ref[:, sl] = z[:, _C_SQ + h * HEAD_DIM:_C_SQ + (h + 1) * HEAD_DIM]
        foxq_ref[:, sl] = _rms(z[:, _C_FQ + h * HEAD_DIM:_C_FQ + (h + 1) * HEAD_DIM], gain(9))
    for i, (col, ref) in enumerate(((_C_NCMP, cmp_ref), (_C_NSEL, sel_ref), (_C_NWIN, win_ref_o))):
        ref[:, :HEAD_DIM] = _rope128(_rms(z[:, col:col + HEAD_DIM], gain(6 + i)), c1, s1)
        ref[:, HEAD_DIM:] = z[:, col + HEAD_DIM:col + 2 * HEAD_DIM]
    sbrow_ref[...] = z[:, _C_SKV:_C_SKV + 2 * HEAD_DIM]
    foxrow_ref[:, :HEAD_DIM] = _rms(z[:, _C_FKV:_C_FKV + HEAD_DIM], gain(10))
    foxrow_ref[:, HEAD_DIM:] = z[:, _C_FKV + HEAD_DIM:_C_FKV + 2 * HEAD_DIM]
    gt = z[:, _C_GATE:_C_GATE + LANES]
    gate_ref[...] = jax.nn.sigmoid(gt)
    logf_ref[...] = _log_sigmoid(gt + fb_ref[...])


_PROJ_OUT_WIDTHS = (4 * MLA_PAD, MLA_PAD, GROUP_WIDTH, 256, 256, 256, GROUP_WIDTH, 256, GROUP_WIDTH, 256,
                    LANES, LANES)


def _proj(x, gin, win, wuq, wuk, vec, fb, tabs, tm):
    t, d = x.shape
    row = lambda w: pl.BlockSpec((tm, w), lambda i: (i, 0))
    full = lambda a: pl.BlockSpec(a.shape, lambda i: (0,) * a.ndim)
    return pl.pallas_call(
        _proj_body,
        grid=(t // tm,),
        in_specs=[row(d), full(gin), full(win), full(wuq), full(wuk), full(vec), full(fb)]
                 + [row(LANES)] * 5,
        out_specs=[row(w) for w in _PROJ_OUT_WIDTHS],
        out_shape=[jax.ShapeDtypeStruct((t, w), F32) for w in _PROJ_OUT_WIDTHS],
        compiler_params=_cparams("parallel"),
        name="proj",
    )(x, gin, win, wuq, wuk, vec, fb, *tabs)


def _outproj_body(h_ref, olat_ref, oc_ref, os_ref, ow_ref, gate_ref, osb_ref, ofox_ref, wuv_ref, wout_ref, o_ref):
    acc = h_ref[...]
    gate = gate_ref[...]
    olat, oc, os_, ow = olat_ref[...], oc_ref[...], os_ref[...], ow_ref[...]
    mla, nsa = [], []
    for h in range(N_HEADS_GROUP):
        sl = slice(h * HEAD_DIM, (h + 1) * HEAD_DIM)
        mla.append(_bdot(olat[:, sl], wuv_ref[h]))
        nsa.append(gate[:, 3 * h:3 * h + 1] * oc[:, sl] + gate[:, 3 * h + 1:3 * h + 2] * os_[:, sl]
                   + gate[:, 3 * h + 2:3 * h + 3] * ow[:, sl])
    groups = (jnp.concatenate(mla, axis=1), jnp.concatenate(nsa, axis=1), osb_ref[...], ofox_ref[...])
    for g, og in enumerate(groups):
        acc = acc + _bdot(og, wout_ref[g * GROUP_WIDTH:(g + 1) * GROUP_WIDTH, :])
    o_ref[...] = acc


def _outproj(h, olat, oc, os_, ow, gate, osb, ofox, wuv, wout, tm):
    t, d = h.shape
    row = lambda w: pl.BlockSpec((tm, w), lambda i: (i, 0))
    full = lambda a: pl.BlockSpec(a.shape, lambda i: (0,) * a.ndim)
    return pl.pallas_call(
        _outproj_body,
        grid=(t // tm,),
        in_specs=[row(d)] + [row(GROUP_WIDTH)] * 4 + [row(LANES)] + [row(GROUP_WIDTH)] * 2 + [full(wuv), full(wout)],
        out_specs=row(d),
        out_shape=jax.ShapeDtypeStruct((t, d), F32),
        compiler_params=_cparams("parallel"),
        name="outproj",
    )(h, olat, oc, os_, ow, gate, osb, ofox, wuv, wout)


def _split2_dot(a, b01):
    hi = a.astype(BF16)
    lo = (a - hi.astype(F32)).astype(BF16)
    return (jnp.dot(hi, b01, preferred_element_type=F32) + jnp.dot(lo, b01, preferred_element_type=F32))


def _pool_dot(p01, x):
    hi = x.astype(BF16)
    lo = (x - hi.astype(F32)).astype(BF16)
    return (jnp.dot(p01, hi, preferred_element_type=F32) + jnp.dot(p01, lo, preferred_element_type=F32))


def _softplus(z):
    return jnp.maximum(z, 0.0) + jnp.log1p(jnp.exp(-jnp.abs(z)))


def _softmax_update(s, valid, pv, m_ref, l_ref, acc_ref):
    n = s.shape[-1]
    if valid is not None:
        s = jnp.where(valid, s, NEG)
    m_prev = m_ref[...]
    m_new = jnp.maximum(m_prev, jnp.max(s, axis=-1, keepdims=True))
    alpha = jnp.exp(m_prev - m_new)
    p = jnp.exp(s - m_new)
    if valid is not None:
        p = jnp.where(valid, p, 0.0)
    l_ref[...] = alpha * l_ref[...] + jnp.sum(p, axis=-1, keepdims=True)
    rows = acc_ref.shape[0]
    acc_ref[...] = acc_ref[...] * alpha.reshape(rows, 1) + pv(p.reshape(rows, n).astype(BF16))
    m_ref[...] = m_new


def _tri_later(n):
    r = lax.broadcasted_iota(jnp.int32, (n, n), 0)
    c = lax.broadcasted_iota(jnp.int32, (n, n), 1)
    return (r > c).astype(BF16)


def _stick_weights(z, valid, tri, carry_ref, width):
    n = z.shape[-1]
    sp = _softplus(z)
    lf = -sp if valid is None else jnp.where(valid, -sp, 0.0)
    pieces = [lf[:, k * width:(k + 1) * width] for k in range(n // width)]
    run = carry_ref[...]
    between = [None] * len(pieces)
    for k in reversed(range(len(pieces))):
        between[k] = _split2_dot(pieces[k], tri) + run
        run = run + jnp.sum(pieces[k], axis=-1, keepdims=True)
    carry_ref[...] = run
    between = between[0] if len(between) == 1 else jnp.concatenate(between, axis=1)
    a = jnp.exp(z - sp + between)
    return a if valid is None else jnp.where(valid, a, 0.0)


def _attn_prompt_body(*refs, tq, tk, dq, dv, kcols, vcols, scale, mode, window, has_bias, has_sel):
    it = iter(refs)
    q_ref, kv_ref = next(it), next(it)
    rb_ref = next(it) if has_bias else None
    cb_ref = next(it) if has_bias else None
    sel_ref = next(it) if has_sel else None
    o_ref = next(it)
    acc_ref, m_ref, l_ref = next(it), next(it), next(it)
    nh = N_HEADS_GROUP
    strict = mode == "stick"
    q0 = pl.program_id(1) * tq
    q = jnp.concatenate([q_ref[0, :, h * dq:(h + 1) * dq] for h in range(nh)], axis=0)
    q = (q * scale).astype(BF16)
    acc_ref[...] = jnp.zeros_like(acc_ref)
    l_ref[...] = jnp.zeros_like(l_ref)
    m_ref[...] = jnp.full_like(m_ref, NEG)
    delta = lax.broadcasted_iota(jnp.int32, (tq, tk), 0) - lax.broadcasted_iota(jnp.int32, (tq, tk), 1)
    selm = sel_ref[0].astype(BF16) if has_sel else None
    sw = min(tk, 256)
    tri = _tri_later(sw) if strict else None
    n_hi = (q0 + tq - 1) // tk + 1
    lo = 0 if window is None else jnp.maximum(q0 - (window - 1), 0) // tk
    all_masked = has_sel or window is not None
    n_full = lo if all_masked else (q0 + (0 if strict else 1)) // tk

    def block(j, masked):
        off = pl.multiple_of(j * tk, tk)
        kv = kv_ref[0, pl.ds(off, tk), :]
        v = kv[:, vcols[0]:vcols[1]].astype(BF16)
        s = _bdot_nt(q, kv[:, kcols[0]:kcols[1]])
        valid = None
        if masked:
            d = delta + (q0 - off)
            valid = (d > 0) if strict else (d >= 0)
            if window is not None:
                valid = valid & (d < window)
            if has_sel:
                lane_blk = (off + lax.broadcasted_iota(jnp.int32, (LANES, tk), 1)) // NSA_CMP_BLOCK
                expand = (lax.broadcasted_iota(jnp.int32, (LANES, tk), 0) == lane_blk).astype(BF16)
                valid = valid & (jnp.dot(selm, expand, preferred_element_type=F32) > 0.5)
        if strict:
            if masked:
                valid = jnp.broadcast_to(valid[None], (nh, tq, tk)).reshape(nh * tq, tk)
            a = _stick_weights(s, valid, tri, l_ref, sw)
            acc_ref[...] += jnp.dot(a.astype(BF16), v, preferred_element_type=F32)
        else:
            s = s.reshape(nh, tq, tk)
            if has_bias:
                s = s + rb_ref[0] - cb_ref[0, :, j]
            _softmax_update(s, valid[None] if masked else None,
                            lambda p: jnp.dot(p, v, preferred_element_type=F32), m_ref, l_ref, acc_ref)

    def masked_step(jj, carry):
        block(n_hi - 1 - jj, True)
        return carry

    def full_step(jj, carry):
        block(n_full - 1 - jj, False)
        return carry

    lax.fori_loop(0, n_hi - n_full, masked_step, 0)
    if not all_masked:
        lax.fori_loop(0, n_full - lo, full_step, 0)
    if strict:
        out = acc_ref[...]
    else:
        l = l_ref[...].reshape(nh * tq, 1)
        out = acc_ref[...] / jnp.where(l > 0.0, l, 1.0)
    for h in range(nh):
        o_ref[0, :, h * dv:(h + 1) * dv] = out[h * tq:(h + 1) * tq]


def _attn_prompt(q, kv, *, dq, dv, kcols, vcols, scale, mode="softmax", window=None, bias=None, sel=None,
                 tq=128, tk=512):
    b, t, _ = q.shape
    nh = N_HEADS_GROUP
    c = kv.shape[-1]
    in_specs = [pl.BlockSpec((1, tq, nh * dq), lambda bi, i: (bi, i, 0)),
                pl.BlockSpec((1, t, c), lambda bi, i: (bi, 0, 0))]
    args = [q, kv]
    if bias is not None:
        in_specs += [pl.BlockSpec((1, nh, tq, 1), lambda bi, i: (bi, 0, i, 0)),
                     pl.BlockSpec((1, nh, t // tk, 1, tk), lambda bi, i: (bi, 0, 0, 0, 0))]
        args += list(bias)
    if sel is not None:
        in_specs.append(pl.BlockSpec((1, tq, LANES), lambda bi, i: (bi, i, 0)))
        args.append(sel)
    body = functools.partial(_attn_prompt_body, tq=tq, tk=tk, dq=dq, dv=dv, kcols=kcols, vcols=vcols, scale=scale,
                             mode=mode, window=window, has_bias=bias is not None, has_sel=sel is not None)
    state = (nh * tq, 1) if mode == "stick" else (nh, tq, 1)
    return pl.pallas_call(
        body,
        grid=(b, t // tq),
        in_specs=in_specs,
        out_specs=pl.BlockSpec((1, tq, nh * dv), lambda bi, i: (bi, i, 0)),
        out_shape=jax.ShapeDtypeStruct((b, t, nh * dv), F32),
        scratch_shapes=[pltpu.VMEM((nh * tq, dv), F32), pltpu.VMEM(state, F32), pltpu.VMEM(state, F32)],
        compiler_params=_cparams("parallel", "arbitrary"),
        name="attn_prompt_" + mode + ("_b" if bias is not None else "") + ("_s" if sel is not None else "")
             + ("_w" if window is not None else ""),
    )(*args)


def _compress_prompt_body(x_ref, o_ref, *, rows):
    nb = rows // NSA_CMP_BLOCK
    pool = (lax.broadcasted_iota(jnp.int32, (nb, rows), 1) // NSA_CMP_BLOCK
            == lax.broadcasted_iota(jnp.int32, (nb, rows), 0)).astype(BF16)
    o_ref[0] = _pool_dot(pool, x_ref[0]) * (1.0 / NSA_CMP_BLOCK)


def _compress_prompt(x, rows=1024):
    b, t, c = x.shape
    return pl.pallas_call(
        functools.partial(_compress_prompt_body, rows=rows),
        grid=(b, t // rows),
        in_specs=[pl.BlockSpec((1, rows, c), lambda bi, i: (bi, i, 0))],
        out_specs=pl.BlockSpec((1, rows // NSA_CMP_BLOCK, c), lambda bi, i: (bi, i, 0)),
        out_shape=jax.ShapeDtypeStruct((b, t // NSA_CMP_BLOCK, c), F32),
        compiler_params=_cparams("parallel", "parallel"),
        name="compress_prompt",
    )(x)


def _topk_mask(score, lanes, k):
    lane = lax.broadcasted_iota(jnp.int32, score.shape, score.ndim - 1)
    rank = jnp.zeros(score.shape, F32)
    for c in lanes:
        col = score[:, c:c + 1]
        tie = jnp.where(lane > c, 1.0, 0.0)
        rank = rank + jnp.where(col > score, 1.0, jnp.where(col == score, tie, 0.0))
    return jnp.where(rank < k, 1.0, 0.0)


def _cmp_core(q, cmp, qpos, n_lanes, rows_per_head, k_sel):
    nh, r = N_HEADS_GROUP, rows_per_head
    lane = lax.broadcasted_iota(jnp.int32, (r, n_lanes), 1)
    s = _bdot_nt(q, cmp[:, :HEAD_DIM]) * (HEAD_DIM ** -0.5)
    s = s.reshape(nh, r, n_lanes)
    valid = (lane * NSA_CMP_BLOCK + (NSA_CMP_BLOCK - 1) <= qpos)[None]
    s = jnp.where(valid, s, NEG)
    m = jnp.max(s, axis=-1, keepdims=True)
    e = jnp.where(valid, jnp.exp(s - m), 0.0)
    den = jnp.sum(e, axis=-1, keepdims=True)
    p = e / jnp.where(den > 0.0, den, 1.0)
    o_c = _bdot(p.reshape(nh * r, n_lanes), cmp[:, HEAD_DIM:])
    imp = p[0] + p[1] + p[2] + p[3]
    imp = imp + pltpu.roll(imp, n_lanes - 1, 1)
    ratio = NSA_SEL_BLOCK // NSA_CMP_BLOCK
    blk = lane // ratio
    cur = qpos // NSA_SEL_BLOCK
    forced = (blk == 0) | (blk == cur) | (blk == cur - 1)
    future = blk * NSA_SEL_BLOCK > qpos
    score = jnp.where(forced, NSA_FORCED_SCORE, jnp.where(future, -1.0, imp))
    even = (lane % ratio) == 0
    score = jnp.where(even, score, -2.0)
    sel = _topk_mask(score, range(0, n_lanes, ratio), k_sel) * jnp.where(even, 1.0, 0.0)
    return o_c, sel + pltpu.roll(sel, 1, 1)


def _cmp_prompt_body(q_ref, cmp_ref, oc_ref, sel_ref, *, tq):
    nh = N_HEADS_GROUP
    i = pl.program_id(1)
    n_lanes = cmp_ref.shape[1]
    q = jnp.concatenate([q_ref[0, :, h * HEAD_DIM:(h + 1) * HEAD_DIM] for h in range(nh)], axis=0).astype(BF16)
    qpos = i * tq + lax.broadcasted_iota(jnp.int32, (tq, n_lanes), 0)
    o_c, sel = _cmp_core(q, cmp_ref[0], qpos, n_lanes, tq, NSA_TOPK)
    for h in range(nh):
        oc_ref[0, :, h * HEAD_DIM:(h + 1) * HEAD_DIM] = o_c[h * tq:(h + 1) * tq]
    sel_ref[0] = sel


def _cmp_prompt(q, cmp, tq=128):
    b, t, w = q.shape
    n_lanes = cmp.shape[1]
    return pl.pallas_call(
        functools.partial(_cmp_prompt_body, tq=tq),
        grid=(b, t // tq),
        in_specs=[pl.BlockSpec((1, tq, w), lambda bi, i: (bi, i, 0)),
                  pl.BlockSpec((1, n_lanes, 2 * HEAD_DIM), lambda bi, i: (bi, 0, 0))],
        out_specs=[pl.BlockSpec((1, tq, w), lambda bi, i: (bi, i, 0)),
                   pl.BlockSpec((1, tq, n_lanes), lambda bi, i: (bi, i, 0))],
        out_shape=[jax.ShapeDtypeStruct((b, t, w), F32), jax.ShapeDtypeStruct((b, t, n_lanes), F32)],
        compiler_params=_cparams("parallel", "parallel"),
        name="cmp_prompt",
    )(q, cmp)


def _cmp_decode_body(q_ref, cmp_ref, oc_ref, sel_ref, *, past_len, n_new_forced):
    n_lanes = cmp_ref.shape[1]
    qpos = past_len + lax.broadcasted_iota(jnp.int32, (DEC_ROWS, n_lanes), 0)
    o_c, sel = _cmp_core(q_ref[0].astype(BF16), cmp_ref[0], qpos, n_lanes, DEC_ROWS, NSA_TOPK - n_new_forced)
    oc_ref[0] = o_c
    sel_ref[0] = sel


def _cmp_decode(q, cmp, past_len):
    b, r, d = q.shape
    n_lanes = cmp.shape[1]
    return pl.pallas_call(
        functools.partial(_cmp_decode_body, past_len=past_len, n_new_forced=1),
        grid=(b,),
        in_specs=[pl.BlockSpec((1, r, d), lambda bi: (bi, 0, 0)),
                  pl.BlockSpec((1, n_lanes, 2 * HEAD_DIM), lambda bi: (bi, 0, 0))],
        out_specs=[pl.BlockSpec((1, r, d), lambda bi: (bi, 0, 0)),
                   pl.BlockSpec((1, DEC_ROWS, n_lanes), lambda bi: (bi, 0, 0))],
        out_shape=[jax.ShapeDtypeStruct((b, r, d), F32), jax.ShapeDtypeStruct((b, DEC_ROWS, n_lanes), F32)],
        compiler_params=_cparams("parallel"),
        name="cmp_decode",
    )(q, cmp)


def _page_specs(block, pps, nchunks, reverse):
    def spec(k):
        def index(bi, c, pt):
            chunk = (nchunks - 1 - c) if reverse else c
            return (pt[bi, chunk * pps + k],) + (0,) * (len(block) - 1)
        return pl.BlockSpec(block, index)
    return [spec(k) for k in range(pps)]


def _compress_decode_body(pt_ref, *refs, pps):
    del pt_ref
    o_ref = refs[pps]
    pool = (lax.broadcasted_iota(jnp.int32, (DEC_ROWS, PAGE_SIZE), 1) // NSA_CMP_BLOCK
            == lax.broadcasted_iota(jnp.int32, (DEC_ROWS, PAGE_SIZE), 0)).astype(BF16)
    for k in range(pps):
        o_ref[0, k] = _pool_dot(pool, refs[k][0]) * (1.0 / NSA_CMP_BLOCK)


def _compress_decode(pool, page_table, pps=16):
    b, n_pages = page_table.shape
    c = pool.shape[-1]
    nchunks = n_pages // pps
    grid_spec = pltpu.PrefetchScalarGridSpec(
        num_scalar_prefetch=1,
        grid=(b, nchunks),
        in_specs=_page_specs((1, PAGE_SIZE, c), pps, nchunks, False),
        out_specs=pl.BlockSpec((1, pps, DEC_ROWS, c), lambda bi, ci, pt: (bi, ci, 0, 0)),
    )
    return pl.pallas_call(
        functools.partial(_compress_decode_body, pps=pps),
        grid_spec=grid_spec,
        out_shape=jax.ShapeDtypeStruct((b, n_pages, DEC_ROWS, c), F32),
        compiler_params=_cparams("parallel", "parallel"),
        name="compress_decode",
    )(page_table, *([pool] * pps))


def _gather_rows8_body(pt_ref, *refs, pps):
    del pt_ref
    o_ref = refs[pps]
    for k in range(pps):
        o_ref[0, :, k * PAGE_SIZE:(k + 1) * PAGE_SIZE] = refs[k][0]


def _gather_rows8(pool_t, page_table, pps=16):
    b, n_pages = page_table.shape
    nchunks = n_pages // pps
    grid_spec = pltpu.PrefetchScalarGridSpec(
        num_scalar_prefetch=1,
        grid=(b, nchunks),
        in_specs=_page_specs((1, DEC_ROWS, PAGE_SIZE), pps, nchunks, False),
        out_specs=pl.BlockSpec((1, DEC_ROWS, pps * PAGE_SIZE), lambda bi, ci, pt: (bi, 0, ci)),
    )
    return pl.pallas_call(
        functools.partial(_gather_rows8_body, pps=pps),
        grid_spec=grid_spec,
        out_shape=jax.ShapeDtypeStruct((b, DEC_ROWS, n_pages * PAGE_SIZE), F32),
        compiler_params=_cparams("parallel", "parallel"),
        name="gather_logf",
    )(page_table, *([pool_t] * pps))


def _cumsum_body(x_ref, o_ref):
    x = x_ref[0]
    n = x.shape[-1]
    lane = lax.broadcasted_iota(jnp.int32, x.shape, 1)
    sh = 1
    while sh < n:
        x = x + jnp.where(lane >= sh, pltpu.roll(x, sh, 1), 0.0)
        sh *= 2
    o_ref[0] = x


def _cumsum_lanes(x):
    g, r, n = x.shape
    return pl.pallas_call(
        _cumsum_body,
        grid=(g,),
        in_specs=[pl.BlockSpec((1, r, n), lambda i: (i, 0, 0))],
        out_specs=pl.BlockSpec((1, r, n), lambda i: (i, 0, 0)),
        out_shape=jax.ShapeDtypeStruct((g, r, n), F32),
        compiler_params=_cparams("parallel"),
        name="cumsum",
    )(x)


def _attn_decode_body(*refs, pps, kparts, vcols, scale, mode, has_ck, kt=False):
    it = iter(refs)
    next(it)
    q_ref, rb_ref, cm_ref, cmn_ref = next(it), next(it), next(it), next(it)
    ck_ref = next(it) if has_ck else None
    ckn_ref = next(it) if has_ck else None
    new_ref = next(it)
    pages = [next(it) for _ in range(pps)]
    o_ref = next(it)
    acc_ref, m_ref, l_ref = next(it), next(it), next(it)
    nh = N_HEADS_GROUP
    c = pl.program_id(1)
    q = (q_ref[0] * scale).astype(BF16)
    rb = rb_ref[0]
    tri = _tri_later(PAGE_SIZE) if mode == "stick" else None

    def scores(page):
        out = None
        for s0, e0, q0 in kparts:
            if kt:
                term = _bdot(q[:, q0:q0 + e0 - s0], page[0, s0:e0, :])
            else:
                term = _bdot_nt(q[:, q0:q0 + e0 - s0], page[0, :, s0:e0])
            out = term if out is None else out + term
        return out

    def attend(page_refs, cm, ck):
        s = [scores(p) for p in page_refs]
        s = s[0] if len(s) == 1 else jnp.concatenate(s, axis=1)
        cm = jnp.concatenate([cm] * nh, axis=0)
        valid = cm > MASK_CUT

        def pv(w):
            out = None
            for k, p in enumerate(page_refs):
                wk = w[:, k * PAGE_SIZE:(k + 1) * PAGE_SIZE]
                if kt:
                    term = _bdot_nt(wk, p[0, vcols[0]:vcols[1], :])
                else:
                    term = jnp.dot(wk, p[0, :, vcols[0]:vcols[1]].astype(BF16), preferred_element_type=F32)
                out = term if out is None else out + term
            return out

        if mode == "stick":
            a = _stick_weights(s, valid, tri, l_ref, PAGE_SIZE)
            acc_ref[...] += pv(a.astype(BF16))
        else:
            s = s + rb
            if ck is not None:
                s = s - jnp.concatenate([jnp.broadcast_to(ck[h:h + 1], (DEC_ROWS, ck.shape[1])) for h in range(nh)],
                                        axis=0)
            _softmax_update(s, valid, pv, m_ref, l_ref, acc_ref)

    @pl.when(c == 0)
    def _():
        acc_ref[...] = jnp.zeros_like(acc_ref)
        l_ref[...] = jnp.zeros_like(l_ref)
        m_ref[...] = jnp.full_like(m_ref, NEG)
        attend([new_ref], cmn_ref[0], ckn_ref[0] if has_ck else None)

    attend(pages, cm_ref[0], ck_ref[0] if has_ck else None)

    @pl.when(c == pl.num_programs(1) - 1)
    def _():
        if mode == "stick":
            o_ref[0] = acc_ref[...]
        else:
            l = l_ref[...]
            o_ref[0] = acc_ref[...] / jnp.where(l > 0.0, l, 1.0)


def _attn_decode(q, pool, page_table, new_page, rb, cm, ck=None, *, kparts, vcols, scale, mode="softmax", pps=32,
                 kt=False):
    b, r, dq = q.shape
    n_pages = page_table.shape[1]
    page_block = (1,) + pool.shape[1:]
    nchunks = n_pages // pps
    dv = vcols[1] - vcols[0]
    rbb = (lambda bi: bi) if rb.shape[0] > 1 else (lambda bi: 0)
    cmb = (lambda bi: bi) if cm.shape[0] > 1 else (lambda bi: 0)
    in_specs = [
        pl.BlockSpec((1, r, dq), lambda bi, ci, pt: (bi, 0, 0)),
        pl.BlockSpec((1, r, 1), lambda bi, ci, pt: (rbb(bi), 0, 0)),
        pl.BlockSpec((1, DEC_ROWS, pps * PAGE_SIZE), lambda bi, ci, pt: (cmb(bi), 0, nchunks - 1 - ci)),
        pl.BlockSpec((1, DEC_ROWS, PAGE_SIZE), lambda bi, ci, pt: (cmb(bi), 0, n_pages)),
    ]
    args = [q, rb, cm, cm]
    if ck is not None:
        in_specs += [pl.BlockSpec((1, DEC_ROWS, pps * PAGE_SIZE), lambda bi, ci, pt: (bi, 0, nchunks - 1 - ci)),
                     pl.BlockSpec((1, DEC_ROWS, PAGE_SIZE), lambda bi, ci, pt: (bi, 0, n_pages))]
        args += [ck, ck]
    in_specs += [pl.BlockSpec(page_block, lambda bi, ci, pt: (bi, 0, 0))]
    in_specs += _page_specs(page_block, pps, nchunks, True)
    args += [new_page] + [pool] * pps
    grid_spec = pltpu.PrefetchScalarGridSpec(
        num_scalar_prefetch=1,
        grid=(b, nchunks),
        in_specs=in_specs,
        out_specs=pl.BlockSpec((1, r, dv), lambda bi, ci, pt: (bi, 0, 0)),
        scratch_shapes=[pltpu.VMEM((r, dv), F32), pltpu.VMEM((r, 1), F32), pltpu.VMEM((r, 1), F32)],
    )
    body = functools.partial(_attn_decode_body, pps=pps, kparts=kparts, vcols=vcols, scale=scale, mode=mode,
                             has_ck=ck is not None, kt=kt)
    return pl.pallas_call(
        body,
        grid_spec=grid_spec,
        out_shape=jax.ShapeDtypeStruct((b, r, dv), F32),
        compiler_params=_cparams("parallel", "arbitrary"),
        name="attn_decode_" + mode + ("_mla" if len(kparts) > 1 else "") + ("_ck" if ck is not None else ""),
    )(page_table, *args)


_IN_SIZES = (MLA_Q_RANK, MLA_KV_RANK, MLA_ROPE, GROUP_WIDTH, 2 * HEAD_DIM, 2 * HEAD_DIM, 2 * HEAD_DIM,
             3 * N_HEADS_GROUP, GROUP_WIDTH, 2 * HEAD_DIM, GROUP_WIDTH, 2 * HEAD_DIM, N_HEADS_GROUP)
_GATE_LANES = 3 * N_HEADS_GROUP
FFN_TILE = 512


def _tile(n, pref):
    for t in range(min(pref, n), 7, -1):
        if n % t == 0 and t % 8 == 0:
            return t
    raise ValueError(f"no tile for {n}")


def _rope_tables(pos):
    posf = pos.astype(F32)[:, None]

    def cos_sin(half):
        inv = ROPE_THETA ** (-jnp.arange(half, dtype=F32) / half)
        ang = posf * inv[None, :]
        return jnp.cos(ang), jnp.sin(ang)

    c, s = cos_sin(HEAD_DIM // 2)
    c1, s1 = jnp.concatenate([c, c], axis=1), jnp.concatenate([-s, s], axis=1)
    c, s = cos_sin(MLA_ROPE // 2)
    z = jnp.zeros_like(s)
    c2 = jnp.concatenate([c, c, z, z], axis=1)
    s2a = jnp.concatenate([-s, z, z, z], axis=1)
    s2b = jnp.concatenate([z, s, z, z], axis=1)
    return c1, s1, c2, s2a, s2b


def _relayout_w_in(w):
    d = w.shape[0]
    parts, start = [], 0
    for size in _IN_SIZES:
        parts.append(w[:, start:start + size])
        start += size
    dq, dkv, kr, nq, ncmp, nsel, nwin, ngate, sq, skv, fq, fkv, fg = parts
    zeros = lambda n: jnp.zeros((d, n), w.dtype)
    out = jnp.concatenate([dq, dkv, kr, zeros(LANES - MLA_ROPE), nq, ncmp, nsel, nwin, sq, skv, fq, fkv,
                           ngate, fg, zeros(LANES - _GATE_LANES - N_HEADS_GROUP)], axis=1)
    assert out.shape[1] == N_IN_PAD
    return out.astype(BF16)


def _pad_to(a, axis, n):
    pad = [(0, 0)] * a.ndim
    pad[axis] = (0, n - a.shape[axis])
    return jnp.pad(a, pad)


def _gain_rows(rows, width=MLA_Q_RANK, n_rows=16):
    out = jnp.stack([_pad_to(r, 0, width) for r in rows], axis=0)
    return _pad_to(out, 0, n_rows)


def kernel(x_prompt, x_sample, cache_mla, cache_nsa_cmp, cache_nsa_sel, state_nsa_win, cache_sb, cache_fox_kv,
           cache_fox_logf, page_table, attn_norm, ffn_norm, ffn_w_in, ffn_w_out, w_in, w_out, mla_q_norm, mla_w_uq,
           mla_qk_norm, mla_kv_norm, mla_kr_norm, mla_w_uk, mla_w_uv, nsa_q_norm, nsa_k_norm, fox_q_norm, fox_k_norm,
           fox_f_bias):
    nb, t, d = x_prompt.shape
    db, ts, _ = x_sample.shape
    depth = w_in.shape[0]
    n_pages = page_table.shape[1]
    past = n_pages * PAGE_SIZE
    n_p, n_s = nb * t, db * ts
    nh = N_HEADS_GROUP
    assert ts <= DEC_ROWS and ts <= NSA_SEL_BLOCK and past % NSA_SEL_BLOCK == 0 and t % LANES == 0
    wb = state_nsa_win.shape[2]
    assert wb == NSA_WINDOW and wb % PAGE_SIZE == 0
    d_ff = ffn_w_out.shape[2]
    ffp = -(-d_ff // FFN_TILE) * FFN_TILE
    page_table = page_table.astype(jnp.int32)

    x = jnp.concatenate([x_prompt.reshape(n_p, d), x_sample.reshape(n_s, d)], axis=0)
    n_t = n_p + n_s
    pos = jnp.concatenate([jnp.tile(jnp.arange(t, dtype=jnp.int32), nb),
                           jnp.tile(past + jnp.arange(ts, dtype=jnp.int32), db)])
    tabs = _rope_tables(pos)
    tm_ffn, tm_proj, tq = _tile(n_t, 512), _tile(n_t, 256), LANES

    n_k = (n_pages + 1) * PAGE_SIZE
    kpos = jnp.arange(n_k, dtype=jnp.int32)[None, :]
    qpos8 = past + jnp.arange(DEC_ROWS, dtype=jnp.int32)[:, None]
    causal8 = kpos <= qpos8
    as_bias = lambda m: jnp.where(m, 0.0, NEG).astype(F32)
    cm_causal = as_bias(causal8)[None]
    cm_strict = as_bias(kpos < qpos8)[None]
    wpos = (past - wb + jnp.arange(wb + PAGE_SIZE, dtype=jnp.int32))[None, :]
    wdiff = qpos8 - wpos
    cm_window = as_bias((wdiff >= 0) & (wdiff < NSA_WINDOW))[None]
    rb_zero = jnp.zeros((1, nh * DEC_ROWS, 1), F32)
    n_pool = cache_mla.shape[1]
    flat = lambda a: a.reshape((a.shape[0] * a.shape[1],) + a.shape[2:])
    pool_mla, pool_cmp, pool_sel = flat(jnp.swapaxes(cache_mla, 2, 3)), flat(cache_nsa_cmp), flat(cache_nsa_sel)
    pool_sb, pool_fox = flat(cache_sb), flat(cache_fox_kv)
    win_pages = wb // PAGE_SIZE
    pool_win = state_nsa_win.reshape(depth * db * win_pages, PAGE_SIZE, 2 * HEAD_DIM)
    win_table = jnp.arange(db * win_pages, dtype=jnp.int32).reshape(db, win_pages)
    pool_logf = flat(_pad_to(jnp.transpose(cache_fox_logf, (0, 1, 3, 2)), 2, DEC_ROWS))

    def dec_q(a, dq):
        a = a[n_p:].reshape(db, ts, nh, dq).transpose(0, 2, 1, 3)
        return _pad_to(a, 2, DEC_ROWS).reshape(db, nh * DEC_ROWS, dq)

    def dec_out(o):
        dv = o.shape[-1]
        return o.reshape(db, nh, DEC_ROWS, dv)[:, :, :ts].transpose(0, 2, 1, 3).reshape(n_s, nh * dv)

    def prompt(a):
        return a[:n_p].reshape(nb, t, a.shape[-1])

    def sample(a):
        return a[n_p:].reshape(db, ts, a.shape[-1])

    def new_page(a):
        return _pad_to(sample(a), 1, PAGE_SIZE)

    st_p = [[] for _ in range(7)]
    st_s = [[] for _ in range(7)]
    scale = HEAD_DIM ** -0.5
    kv_std = dict(kparts=((0, HEAD_DIM, 0),), vcols=(HEAD_DIM, 2 * HEAD_DIM), scale=scale)
    for l in range(depth):
        def ffn(x, i):
            wa = _pad_to(ffn_w_in[l, i][:, :d_ff], 1, ffp).astype(BF16)
            wb_ = _pad_to(ffn_w_in[l, i][:, d_ff:], 1, ffp).astype(BF16)
            wo = _pad_to(ffn_w_out[l, i], 0, ffp).astype(BF16)
            return _ffn(x, ffn_norm[l, i][None, :], wa, wb_, wo, tm_ffn, FFN_TILE)

        x = ffn(x, 0)
        wuq = _pad_to(mla_w_uq[l].reshape(MLA_Q_RANK, nh, MLA_QK), 2, MLA_PAD).reshape(MLA_Q_RANK, nh * MLA_PAD)
        vec = _gain_rows([mla_q_norm[l], mla_qk_norm[l, :MLA_NOPE], mla_qk_norm[l, MLA_NOPE:], mla_kv_norm[l],
                          mla_kr_norm[l], nsa_q_norm[l], nsa_k_norm[l, 0], nsa_k_norm[l, 1], nsa_k_norm[l, 2],
                          fox_q_norm[l], fox_k_norm[l]])
        fb = jnp.zeros((1, LANES), F32).at[0, _GATE_LANES:_GATE_LANES + nh].set(fox_f_bias[l])
        (mlaq, mlarow, nsaq, cmprow, selrow, winrow, sbq, sbrow, foxq, foxrow, gate, logf_t) = _proj(
            x, attn_norm[l][None, :], _relayout_w_in(w_in[l]), wuq.astype(BF16),
            jnp.transpose(mla_w_uk[l], (0, 2, 1)).astype(BF16), vec, fb, tabs, tm_proj)
        mlarow = mlarow[:, :MLA_QK]
        logf = logf_t[:, _GATE_LANES:_GATE_LANES + nh]

        rows_bf = lambda a: prompt(a).astype(BF16)
        olat_p = _attn_prompt(prompt(mlaq), _pad_to(prompt(mlarow), 2, MLA_PAD).astype(BF16), dq=MLA_PAD,
                              dv=MLA_KV_RANK, kcols=(0, MLA_PAD), vcols=(0, MLA_KV_RANK), scale=MLA_QK ** -0.5,
                              tq=tq, tk=_tile(t, 1024))
        std = dict(dq=HEAD_DIM, dv=HEAD_DIM, kcols=(0, HEAD_DIM), vcols=(HEAD_DIM, 2 * HEAD_DIM), scale=scale, tq=tq)
        cmp_p = _compress_prompt(prompt(cmprow), rows=min(1024, t))
        oc_p, selm_p = _cmp_prompt(prompt(nsaq), cmp_p, tq=tq)
        os_p = _attn_prompt(prompt(nsaq), rows_bf(selrow), sel=selm_p, tk=_tile(t, 1024), **std)
        ow_p = _attn_prompt(prompt(nsaq), rows_bf(winrow), window=NSA_WINDOW, tk=_tile(t, 256), **std)
        osb_p = _attn_prompt(prompt(sbq), rows_bf(sbrow), mode="stick", tk=_tile(t, 1024), **std)
        lf_p = _pad_to(jnp.transpose(prompt(logf), (0, 2, 1)), 1, DEC_ROWS)
        cum_p = _cumsum_lanes(lf_p)[:, :nh]
        tk_fox = _tile(t, 1024)
        ofox_p = _attn_prompt(prompt(foxq), rows_bf(foxrow), tk=tk_fox,
                              bias=(cum_p[..., None], cum_p.reshape(nb, nh, t // tk_fox, 1, tk_fox)), **std)

        pt_l = page_table + l * n_pool
        olat_s = _attn_decode(dec_q(mlaq, MLA_PAD), pool_mla, pt_l, jnp.swapaxes(new_page(mlarow), 1, 2), rb_zero,
                              cm_causal, kparts=((0, MLA_KV_RANK, 0), (MLA_KV_RANK, MLA_QK, MLA_KV_RANK)),
                              vcols=(0, MLA_KV_RANK), scale=MLA_QK ** -0.5, kt=True)
        nsaq_s = dec_q(nsaq, HEAD_DIM)
        cmp_s = _compress_decode(pool_cmp, pt_l)[:, :, :PAGE_SIZE // NSA_CMP_BLOCK]
        cmp_s = cmp_s.reshape(db, past // NSA_CMP_BLOCK, 2 * HEAD_DIM)
        oc_s, selm_s = _cmp_decode(nsaq_s, cmp_s, past)
        sel_keys = jnp.repeat(selm_s, NSA_CMP_BLOCK, axis=-1) > 0.5
        sel_keys = jnp.concatenate([sel_keys, jnp.ones((db, DEC_ROWS, PAGE_SIZE), bool)], axis=-1)
        os_s = _attn_decode(nsaq_s, pool_sel, pt_l, new_page(selrow), rb_zero, as_bias(sel_keys & causal8[None]),
                            **kv_std)
        ow_s = _attn_decode(nsaq_s, pool_win, win_table + l * db * win_pages, new_page(winrow), rb_zero, cm_window,
                            pps=win_pages, **kv_std)
        osb_s = _attn_decode(dec_q(sbq, HEAD_DIM), pool_sb, pt_l, new_page(sbrow), rb_zero, cm_strict,
                             mode="stick", **kv_std)
        lf_new = _pad_to(_pad_to(jnp.transpose(sample(logf), (0, 2, 1)), 1, DEC_ROWS), 2, PAGE_SIZE)
        cum_s = _cumsum_lanes(jnp.concatenate([_gather_rows8(pool_logf, pt_l), lf_new], axis=-1))
        cq = _pad_to(cum_s[:, :nh, past:past + ts], 2, DEC_ROWS).reshape(db, nh * DEC_ROWS, 1)
        ofox_s = _attn_decode(dec_q(foxq, HEAD_DIM), pool_fox, pt_l, new_page(foxrow), cq, cm_causal, cum_s,
                              **kv_std)

        both = lambda p, s: jnp.concatenate([p.reshape(n_p, p.shape[-1]), dec_out(s)], axis=0)
        x = _outproj(x, both(olat_p, olat_s), both(oc_p, oc_s), both(os_p, os_s), both(ow_p, ow_s), gate,
                     both(osb_p, osb_s), both(ofox_p, ofox_s), mla_w_uv[l].astype(BF16), w_out[l].astype(BF16),
                     tm_ffn)
        x = ffn(x, 1)

        wp = min(NSA_WINDOW, t)
        win_s = jnp.concatenate([state_nsa_win[l], sample(winrow)], axis=1)[:, -wb:]
        for lst, a in zip(st_p, (prompt(mlarow), prompt(cmprow), prompt(selrow), prompt(winrow)[:, -wp:],
                                 prompt(sbrow), prompt(foxrow), prompt(logf))):
            lst.append(a)
        for lst, a in zip(st_s, (sample(mlarow), sample(cmprow), sample(selrow), win_s,
                                 sample(sbrow), sample(foxrow), sample(logf))):
            lst.append(a)

    outs_p = [jnp.stack(s, axis=0) for s in st_p]
    outs_s = [jnp.stack(s, axis=0) for s in st_s]
    res = [x[:n_p].reshape(nb, t, d), x[n_p:].reshape(db, ts, d)]
    for p, s in zip(outs_p, outs_s):
        res += [p, s]
    return tuple(res)
```
